```python
import jax, jax.numpy as jnp
from jax import lax
import numpy as np

D_MODEL = 1024
BATCH = 16
SEQ = 2048
DEPTH = 4

SB_HEADS = 8
SB_HEAD_DIM = 64
SB_WIDTH = SB_HEADS * SB_HEAD_DIM
SB_BLOCK = 128
DN_HEADS = 4
DN_HEAD_DIM = 128
DN_WIDTH = DN_HEADS * DN_HEAD_DIM
DN_CHUNK = 64
CONV_WIDTH = 4
MIX_WIDTH = SB_WIDTH + DN_WIDTH
IN_WIDTH = 3 * SB_WIDTH + 4 * DN_WIDTH + 2 * DN_HEADS
D_FF = 4 * D_MODEL
N_MOD = 6
EPS = 1e-6

kernel_name = "stickbreak_gdn_hybrid_adaln"


def rmsnorm(x, gain):
    xf = x.astype(jnp.float32)
    y = xf * lax.rsqrt(jnp.mean(xf * xf, axis=-1, keepdims=True) + EPS)
    return (y * gain.astype(jnp.float32)).astype(x.dtype)


def l2norm(x):
    return x * lax.rsqrt(jnp.sum(x * x, axis=-1, keepdims=True) + EPS)


def causal_depthwise_conv(x, w):
    k_w, ch = w.shape
    return lax.conv_general_dilated(
        x, w[:, None, :].astype(x.dtype), window_strides=(1,), padding=((k_w - 1, 0),),
        dimension_numbers=("NWC", "WIO", "NWC"), feature_group_count=ch)


def stick_breaking_attention(q, k, v):
    t_len = q.shape[2]
    scale = SB_HEAD_DIM ** -0.5
    outs = []
    for blk in range(t_len // SB_BLOCK):
        start = blk * SB_BLOCK
        end = start + SB_BLOCK
        qb = q[:, :, start:end]
        kb = k[:, :, :end]
        vb = v[:, :, :end]
        z = jnp.einsum("bhqd,bhkd->bhqk", qb, kb, preferred_element_type=jnp.float32) * scale
        t_idx = start + jnp.arange(SB_BLOCK)[:, None]
        s_idx = jnp.arange(end)[None, :]
        causal = s_idx < t_idx
        log_1m_beta = jnp.where(causal, jax.nn.log_sigmoid(-z), 0.0)
        suffix = lax.cumsum(log_1m_beta, axis=3, reverse=True) - log_1m_beta
        att = jnp.where(causal, jnp.exp(jax.nn.log_sigmoid(z) + suffix), 0.0)
        outs.append(jnp.einsum("bhqk,bhkd->bhqd", att.astype(vb.dtype), vb))
    return jnp.concatenate(outs, axis=2)


def chunk_gated_delta_rule(q, k, v, g, beta):
    b, h, t_len, dk = q.shape
    dv = v.shape[-1]
    n = t_len // DN_CHUNK
    c = DN_CHUNK
    q = q * dk ** -0.5
    qc = q.reshape(b, h, n, c, dk)
    kc = k.reshape(b, h, n, c, dk)
    vc = v.reshape(b, h, n, c, dv)
    bc = beta.reshape(b, h, n, c)
    gc = jnp.cumsum(g.reshape(b, h, n, c), axis=-1)
    tril_incl = jnp.tril(jnp.ones((c, c), dtype=bool))
    tril_strict = jnp.tril(jnp.ones((c, c), dtype=bool), -1)
    diff = jnp.where(tril_incl, gc[..., :, None] - gc[..., None, :], 0.0)
    decay = jnp.where(tril_incl, jnp.exp(diff), 0.0)
    k_beta = kc * bc[..., None]
    v_beta = vc * bc[..., None]
    lower = jnp.where(tril_strict, jnp.einsum("bhnid,bhnjd->bhnij", k_beta, kc) * decay, 0.0)
    eye = jnp.eye(c, dtype=jnp.float32)
    a_mat = lower + eye
    t_mat = lax.linalg.triangular_solve(a_mat, jnp.broadcast_to(eye, a_mat.shape),
                                        left_side=True, lower=True, unit_diagonal=True)
    u = jnp.einsum("bhnij,bhnje->bhnie", t_mat, v_beta)
    w = jnp.einsum("bhnij,bhnjd->bhnid", t_mat, k_beta * jnp.exp(gc)[..., None])
    intra = jnp.where(tril_incl, jnp.einsum("bhnid,bhnjd->bhnij", qc, kc) * decay, 0.0)

    def step(state, inp):
        q_n, k_n, u_n, w_n, g_n, a_n = inp
        v_new = u_n - jnp.einsum("bhcd,bhde->bhce", w_n, state)
        o_n = (jnp.einsum("bhcd,bhde->bhce", q_n * jnp.exp(g_n)[..., None], state)
               + jnp.einsum("bhij,bhje->bhie", a_n, v_new))
        g_last = g_n[..., -1:]
        state = (state * jnp.exp(g_last)[..., None]
                 + jnp.einsum("bhcd,bhce->bhde", k_n * jnp.exp(g_last - g_n)[..., None], v_new))
        return state, o_n

    xs = tuple(jnp.moveaxis(a, 2, 0) for a in (qc, kc, u, w, gc, intra))
    state0 = jnp.zeros((b, h, dk, dv), jnp.float32)
    _, o = lax.scan(step, state0, xs)
    return jnp.moveaxis(o, 0, 2).reshape(b, h, t_len, dv)


def hybrid_mixer(h, w_in, sb_q_norm, sb_k_norm, conv_w, a_log, dt_bias, dn_out_norm, w_out):
    b, t_len, _ = h.shape
    proj = h @ w_in
    cuts = [int(i) for i in np.cumsum([SB_WIDTH, SB_WIDTH, SB_WIDTH, 3 * DN_WIDTH, DN_WIDTH, DN_HEADS])]
    sb_q, sb_k, sb_v, dn_qkv, dn_z, dn_a, dn_b = jnp.split(proj, cuts, axis=-1)

    def sb_heads(a):
        return a.reshape(b, t_len, SB_HEADS, SB_HEAD_DIM)
    q_a = rmsnorm(sb_heads(sb_q), sb_q_norm).transpose(0, 2, 1, 3)
    k_a = rmsnorm(sb_heads(sb_k), sb_k_norm).transpose(0, 2, 1, 3)
    v_a = sb_heads(sb_v).transpose(0, 2, 1, 3)
    o_a = stick_breaking_attention(q_a, k_a, v_a)
    o_a = o_a.transpose(0, 2, 1, 3).reshape(b, t_len, SB_WIDTH)

    qkv = jax.nn.silu(causal_depthwise_conv(dn_qkv, conv_w)).astype(jnp.float32)
    q_b, k_b, v_b = jnp.split(qkv, 3, axis=-1)
    def dn_heads(a):
        return a.reshape(b, t_len, DN_HEADS, DN_HEAD_DIM).transpose(0, 2, 1, 3)
    q_b = l2norm(dn_heads(q_b))
    k_b = l2norm(dn_heads(k_b))
    v_b = dn_heads(v_b)
    beta = jax.nn.sigmoid(dn_b.astype(jnp.float32)).transpose(0, 2, 1)
    g = (-jnp.exp(a_log.astype(jnp.float32))
         * jax.nn.softplus(dn_a.astype(jnp.float32) + dt_bias.astype(jnp.float32))).transpose(0, 2, 1)
    o_b = chunk_gated_delta_rule(q_b, k_b, v_b, g, beta)
    o_b = o_b.transpose(0, 2, 1, 3).astype(h.dtype)
    z = dn_z.reshape(b, t_len, DN_HEADS, DN_HEAD_DIM)
    o_b = (rmsnorm(o_b, dn_out_norm) * jax.nn.silu(z)).reshape(b, t_len, DN_WIDTH)

    return jnp.concatenate([o_a, o_b], axis=-1) @ w_out


def setup_inputs(seed: int = 0) -> dict:
    key = jax.random.key(seed)
    ks = jax.random.split(key, 20)
    f32 = jnp.float32
    x = jax.random.normal(ks[0], (BATCH, SEQ, D_MODEL), f32)
    c = jax.random.normal(ks[1], (BATCH, D_MODEL), f32)
    w_ada = jax.random.normal(ks[2], (DEPTH, D_MODEL, N_MOD * D_MODEL), f32) * (0.5 * D_MODEL ** -0.5)
    b_ada = jax.random.normal(ks[3], (DEPTH, N_MOD * D_MODEL), f32) * 0.01
    norm_mix = 1.0 + 0.02 * jax.random.normal(ks[4], (DEPTH, D_MODEL), f32)
    norm_mlp = 1.0 + 0.02 * jax.random.normal(ks[5], (DEPTH, D_MODEL), f32)
    w_in = jax.random.normal(ks[6], (DEPTH, D_MODEL, IN_WIDTH), f32) * D_MODEL ** -0.5
    sb_q_norm = 1.0 + 0.02 * jax.random.normal(ks[7], (DEPTH, SB_HEAD_DIM), f32)
    sb_k_norm = 1.0 + 0.02 * jax.random.normal(ks[8], (DEPTH, SB_HEAD_DIM), f32)
    conv_w = jax.random.normal(ks[9], (DEPTH, CONV_WIDTH, 3 * DN_WIDTH), f32) * CONV_WIDTH ** -0.5
    a_log = jnp.log(jax.random.uniform(ks[10], (DEPTH, DN_HEADS), f32, 1.0, 16.0))
    dt = jnp.exp(jax.random.uniform(ks[11], (DEPTH, DN_HEADS), f32, np.log(1e-3), np.log(1e-1)))
    dt_bias = dt + jnp.log(-jnp.expm1(-dt))
    dn_out_norm = 1.0 + 0.02 * jax.random.normal(ks[12], (DEPTH, DN_HEAD_DIM), f32)
    w_out = jax.random.normal(ks[13], (DEPTH, MIX_WIDTH, D_MODEL), f32) * MIX_WIDTH ** -0.5
    w_ff1 = jax.random.normal(ks[14], (DEPTH, D_MODEL, D_FF), f32) * D_MODEL ** -0.5
    w_ff2 = jax.random.normal(ks[15], (DEPTH, D_FF, D_MODEL), f32) * D_FF ** -0.5
    return {"x": x, "c": c, "w_ada": w_ada, "b_ada": b_ada, "norm_mix": norm_mix,
            "norm_mlp": norm_mlp, "w_in": w_in, "sb_q_norm": sb_q_norm, "sb_k_norm": sb_k_norm,
            "conv_w": conv_w, "a_log": a_log, "dt_bias": dt_bias, "dn_out_norm": dn_out_norm,
            "w_out": w_out, "w_ff1": w_ff1, "w_ff2": w_ff2}


def reference(x, c, w_ada, b_ada, norm_mix, norm_mlp, w_in, sb_q_norm, sb_k_norm,
              conv_w, a_log, dt_bias, dn_out_norm, w_out, w_ff1, w_ff2):
    cond = jax.nn.silu(c)
    for l in range(DEPTH):
        mod = cond @ w_ada[l] + b_ada[l]
        sh_a, sc_a, g_a, sh_m, sc_m, g_m = [m[:, None, :] for m in jnp.split(mod, N_MOD, axis=-1)]
        h = rmsnorm(x, norm_mix[l]) * (1.0 + sc_a) + sh_a
        x = x + g_a * hybrid_mixer(h, w_in[l], sb_q_norm[l], sb_k_norm[l], conv_w[l],
                                   a_log[l], dt_bias[l], dn_out_norm[l], w_out[l])
        h = rmsnorm(x, norm_mlp[l]) * (1.0 + sc_m) + sh_m
        x = x + g_m * (jnp.square(jax.nn.relu(h @ w_ff1[l])) @ w_ff2[l])
    return x
```

```python
import functools

import jax
import jax.numpy as jnp
from jax import lax
from jax.experimental import pallas as pl
from jax.experimental.pallas import tpu as pltpu

F32 = jnp.float32
BF16 = jnp.bfloat16
EPS = 1e-6

SB_HEADS = 8
SB_HEAD_DIM = 64
SB_WIDTH = SB_HEADS * SB_HEAD_DIM
DN_HEADS = 4
DN_HEAD_DIM = 128
DN_WIDTH = DN_HEADS * DN_HEAD_DIM
DN_CHUNK = 64
CONV_WIDTH = 4
N_MOD = 6
LANES = 128
AB_PAD = LANES
VMEM_LIMIT = 56 * 1024 * 1024

NN = (((1,), (0,)), ((), ()))
NT = (((1,), (1,)), ((), ()))
TN = (((0,), (0,)), ((), ()))


def _dot(a, b, dims=NN):
    return lax.dot_general(a, b, dims, preferred_element_type=F32)


def _split2(a):
    hi = a.astype(BF16)
    lo = (a - hi.astype(F32)).astype(BF16)
    return hi, lo


def _split3(a):
    hi = a.astype(BF16)
    r = a - hi.astype(F32)
    mid = r.astype(BF16)
    lo = (r - mid.astype(F32)).astype(BF16)
    return hi, mid, lo


def _dot_x3(a, b, dims=NN):
    ah, al = _split2(a)
    bh, bl = _split2(b)
    return _dot(ah, bh, dims) + (_dot(ah, bl, dims) + _dot(al, bh, dims))


def _dot_exact_rhs(a, m_bf16):
    h, m, l = _split3(a)
    return _dot(h, m_bf16) + (_dot(m, m_bf16) + _dot(l, m_bf16))


def _sigmoid(x):
    return 1.0 / (1.0 + jnp.exp(-x))


def _softplus(x):
    return jnp.maximum(x, 0.0) + jnp.log1p(jnp.exp(-jnp.abs(x)))


def _ada_kernel(c_ref, w_ref, b_ref, o_ref):
    c = c_ref[...]
    cond = c * _sigmoid(c)
    o_ref[0] = _dot_x3(cond, w_ref[0]) + b_ref[0]


def _ada_call(c, w_ada, b_ada):
    depth, d, n = w_ada.shape
    b = c.shape[0]
    tn = 1536
    return pl.pallas_call(
        _ada_kernel,
        grid=(depth, n // tn),
        in_specs=[
            pl.BlockSpec((b, d), lambda l, j: (0, 0)),
            pl.BlockSpec((1, d, tn), lambda l, j: (l, 0, j)),
            pl.BlockSpec((1, 1, tn), lambda l, j: (l, 0, j)),
        ],
        out_specs=pl.BlockSpec((1, b, tn), lambda l, j: (l, 0, j)),
        out_shape=jax.ShapeDtypeStruct((depth, b, n), F32),
        compiler_params=pltpu.CompilerParams(
            dimension_semantics=("arbitrary", "arbitrary"), vmem_limit_bytes=VMEM_LIMIT),
        name="adaln_mod",
    )(c, w_ada, b_ada.reshape(depth, 1, n))


def _premix_kernel(x_ref, mod_ref, gain_ref, wsb_ref, wdn_ref, wz_ref, wab_ref, qg_ref, kg_ref,
                   blk_ref, alog_ref, dtb_ref, cum_ref,
                   q_ref, k_ref, v_ref, dn_ref, z_ref, gb_ref):
    x = x_ref[0]
    ms = jnp.mean(x * x, axis=-1, keepdims=True)
    y = x * lax.rsqrt(ms + EPS) * gain_ref[...]
    h = y * (1.0 + mod_ref[1:2, :]) + mod_ref[0:1, :]
    hb = h.astype(BF16)

    sb = _dot(hb, wsb_ref[...])
    blk = blk_ref[...]

    def head_rmsnorm(a, g):
        sh, sl = _split2(a * a)
        m = _dot(sh, blk) + _dot(sl, blk)
        return a * lax.rsqrt(m + EPS) * g

    q = head_rmsnorm(sb[:, :SB_WIDTH], qg_ref[...]) * (SB_HEAD_DIM ** -0.5)
    k = head_rmsnorm(sb[:, SB_WIDTH:2 * SB_WIDTH], kg_ref[...])
    q_ref[0] = q.astype(BF16)
    k_ref[0] = k.astype(BF16)
    v_ref[0] = sb[:, 2 * SB_WIDTH:].astype(BF16)

    dn_ref[0] = _dot(hb, wdn_ref[...])
    z_ref[0] = _dot(hb, wz_ref[...])

    ab = _dot(hb, wab_ref[...])
    ab8 = ab.T[0:2 * DN_HEADS, :]
    row = lax.broadcasted_iota(jnp.int32, ab8.shape, 0)
    g = -jnp.exp(alog_ref[...]) * _softplus(ab8 + dtb_ref[...])
    gc = _dot_exact_rhs(g, cum_ref[...])
    gb_ref[0] = jnp.where(row < DN_HEADS, gc, _sigmoid(ab8))


def _premix_call(x, mod, l, gain, wsb, wdn, wz, wab, qg, kg, blk, alog8, dtb8, cum, tm):
    b, t, d = x.shape
    full = lambda a: pl.BlockSpec(a.shape, lambda bi, i: (0,) * a.ndim)
    tok = lambda w: pl.BlockSpec((1, tm, w), lambda bi, i: (bi, i, 0))
    return pl.pallas_call(
        _premix_kernel,
        grid=(b, t // tm),
        in_specs=[
            tok(d),
            pl.BlockSpec((None, None, N_MOD, d), lambda bi, i: (l, bi, 0, 0)),
            full(gain), full(wsb), full(wdn), full(wz), full(wab), full(qg), full(kg), full(blk),
            full(alog8), full(dtb8), full(cum),
        ],
        out_specs=[tok(SB_WIDTH), tok(SB_WIDTH), tok(SB_WIDTH), tok(3 * DN_WIDTH), tok(DN_WIDTH),
                   pl.BlockSpec((1, 2 * DN_HEADS, tm), lambda bi, i: (bi, 0, i))],
        out_shape=[
            jax.ShapeDtypeStruct((b, t, SB_WIDTH), BF16),
            jax.ShapeDtypeStruct((b, t, SB_WIDTH), BF16),
            jax.ShapeDtypeStruct((b, t, SB_WIDTH), BF16),
            jax.ShapeDtypeStruct((b, t, 3 * DN_WIDTH), F32),
            jax.ShapeDtypeStruct((b, t, DN_WIDTH), F32),
            jax.ShapeDtypeStruct((b, 2 * DN_HEADS, t), F32),
        ],
        compiler_params=pltpu.CompilerParams(
            dimension_semantics=("parallel", "parallel"), vmem_limit_bytes=VMEM_LIMIT),
        name="premix_proj",
    )(x, mod, gain, wsb, wdn, wz, wab, qg, kg, blk, alog8, dtb8, cum)


def _sb_attn_kernel(q_ref, k_ref, v_ref, u_ref, o_ref, *, tq):
    i = pl.program_id(2)
    q = q_ref[0]
    lane = lax.broadcasted_iota(jnp.int32, (1, LANES), 1)
    first = lane < SB_HEAD_DIM
    zero = jnp.zeros_like(q)
    qs = (jnp.where(first, q, zero), jnp.where(first, zero, q))
    u = u_ref[...]
    row = lax.broadcasted_iota(jnp.int32, (tq, tq), 0)
    col = lax.broadcasted_iota(jnp.int32, (tq, tq), 1)
    causal = col < row

    def block(j, carry, masked):
        accs, cs = carry
        start = pl.multiple_of(j * tq, tq)
        kb = k_ref[0, pl.ds(start, tq), :]
        vb = v_ref[0, pl.ds(start, tq), :]
        new_acc, new_c = [], []
        for hh in range(2):
            z = _dot(qs[hh], kb, NT)
            sp = _softplus(z)
            l1m = -sp
            if masked:
                l1m = jnp.where(causal, l1m, 0.0)
            lh, ll = _split2(l1m)
            suffix = _dot(lh, u) + _dot(ll, u)
            att = jnp.exp((z - sp) + suffix + cs[hh])
            if masked:
                att = jnp.where(causal, att, 0.0)
            new_acc.append(accs[hh] + _dot(att.astype(BF16), vb))
            new_c.append(cs[hh] + jnp.sum(l1m, axis=-1, keepdims=True))
        return tuple(new_acc), tuple(new_c)

    acc0 = jnp.zeros((tq, LANES), F32)
    c0 = jnp.zeros((tq, 1), F32)
    carry = block(i, ((acc0, acc0), (c0, c0)), True)
    carry = lax.fori_loop(0, i, lambda jj, cr: block(i - 1 - jj, cr, False), carry)
    accs, _ = carry
    o_ref[0] = jnp.where(first, accs[0], accs[1]).astype(BF16)


def _sb_attn_call(q, k, v, u, tq):
    b, t, _ = q.shape
    return pl.pallas_call(
        functools.partial(_sb_attn_kernel, tq=tq),
        grid=(b, SB_WIDTH // LANES, t // tq),
        in_specs=[
            pl.BlockSpec((1, tq, LANES), lambda bi, p, i: (bi, i, p)),
            pl.BlockSpec((1, t, LANES), lambda bi, p, i: (bi, 0, p)),
            pl.BlockSpec((1, t, LANES), lambda bi, p, i: (bi, 0, p)),
            pl.BlockSpec((tq, tq), lambda bi, p, i: (0, 0)),
        ],
        out_specs=pl.BlockSpec((1, tq, LANES), lambda bi, p, i: (bi, i, p)),
        out_shape=jax.ShapeDtypeStruct((b, t, SB_WIDTH), BF16),
        compiler_params=pltpu.CompilerParams(
            dimension_semantics=("parallel", "parallel", "arbitrary"), vmem_limit_bytes=VMEM_LIMIT),
        name="sb_attention",
    )(q, k, v, u)


def _unit_lower_inverse(a, ri, ci):
    b16 = (ri >> 4) == (ci >> 4)
    b32 = (ri >> 5) == (ci >> 5)
    a16 = jnp.where(b16, a, 0.0)
    x = jnp.where(ri == ci, 1.0, 0.0) - a16
    p = _dot_x3(a16, a16)
    x = x + _dot_x3(x, p)
    p = _dot_x3(p, p)
    x = x + _dot_x3(x, p)
    p = _dot_x3(p, p)
    x = x + _dot_x3(x, p)
    off32 = jnp.where(b32, a - a16, 0.0)
    x = x - _dot_x3(x, _dot_x3(off32, x))
    off64 = jnp.where(b32, 0.0, a)
    x = x - _dot_x3(x, _dot_x3(off64, x))
    return x


def _gdn_kernel(dn_ref, z_ref, gb_ref, cw_ref, on_ref, o_ref,
                xbuf, qkv_s, u_s, w_s, qg_s, kd_s, intra_s, state, *, tt):
    c_sz = DN_CHUNK
    hd = DN_HEAD_DIM

    @pl.when(pl.program_id(1) == 0)
    def _():
        state[...] = jnp.zeros_like(state)
        xbuf[0:8, :] = jnp.zeros((8, 3 * DN_WIDTH), F32)

    xbuf[8:8 + tt, :] = dn_ref[0]
    for cb in range(3 * DN_HEADS):
        ls = slice(cb * hd, (cb + 1) * hd)
        acc = cw_ref[CONV_WIDTH - 1:CONV_WIDTH, ls] * xbuf[8:8 + tt, ls]
        for kk in range(CONV_WIDTH - 1):
            off = 8 - (CONV_WIDTH - 1) + kk
            acc = acc + cw_ref[kk:kk + 1, ls] * xbuf[off:off + tt, ls]
        act = acc * _sigmoid(acc)
        if cb < 2 * DN_HEADS:
            act = act * lax.rsqrt(jnp.sum(act * act, axis=-1, keepdims=True) + EPS)
        if cb < DN_HEADS:
            act = act * (hd ** -0.5)
        qkv_s[:, ls] = act
    xbuf[0:8, :] = xbuf[tt:tt + 8, :]

    ri = lax.broadcasted_iota(jnp.int32, (c_sz, c_sz), 0)
    ci = lax.broadcasted_iota(jnp.int32, (c_sz, c_sz), 1)
    eye = ri == ci
    tril_incl = ci <= ri
    tril_strict = ci < ri

    def to_col(r):
        return jnp.sum(jnp.where(eye, r, 0.0), axis=1, keepdims=True)

    n_chunks = tt // c_sz
    for c in range(n_chunks):
        rs = slice(c * c_sz, (c + 1) * c_sz)
        for h in range(DN_HEADS):
            q = qkv_s[rs, h * hd:(h + 1) * hd]
            k = qkv_s[rs, DN_WIDTH + h * hd:DN_WIDTH + (h + 1) * hd]
            v = qkv_s[rs, 2 * DN_WIDTH + h * hd:2 * DN_WIDTH + (h + 1) * hd]
            gc_row = gb_ref[0, h, c:c + 1, :]
            beta_col = to_col(gb_ref[0, DN_HEADS + h, c:c + 1, :])
            gc_col = to_col(gc_row)
            g_last = gc_row[:, c_sz - 1:c_sz]
            decay = jnp.where(tril_incl, jnp.exp(jnp.where(tril_incl, gc_col - gc_row, 0.0)), 0.0)
            k_beta = k * beta_col
            a = jnp.where(tril_strict, _dot_x3(k_beta, k, NT) * decay, 0.0)
            t_mat = _unit_lower_inverse(a, ri, ci)
            u_s[rs, h * hd:(h + 1) * hd] = _dot_x3(t_mat, v * beta_col)
            w_s[rs, h * hd:(h + 1) * hd] = _dot_x3(t_mat, k_beta * jnp.exp(gc_col))
            intra_s[h, rs, :] = jnp.where(tril_incl, _dot_x3(q, k, NT) * decay, 0.0)
            qg_s[rs, h * hd:(h + 1) * hd] = q * jnp.exp(gc_col)
            kd_s[rs, h * hd:(h + 1) * hd] = k * jnp.exp(g_last - gc_col)

    for c in range(n_chunks):
        rs = slice(c * c_sz, (c + 1) * c_sz)
        for h in range(DN_HEADS):
            hs = slice(h * hd, (h + 1) * hd)
            s = state[h]
            v_new = u_s[rs, hs] - _dot_x3(w_s[rs, hs], s)
            o = _dot_x3(qg_s[rs, hs], s) + _dot_x3(intra_s[h, rs, :], v_new)
            e_last = jnp.exp(gb_ref[0, h, c:c + 1, c_sz - 1:c_sz])
            state[h] = s * e_last + _dot_x3(kd_s[rs, hs], v_new, TN)
            ms = jnp.mean(o * o, axis=-1, keepdims=True)
            zz = z_ref[0, rs, hs]
            out = o * lax.rsqrt(ms + EPS) * on_ref[...] * (zz * _sigmoid(zz))
            o_ref[0, rs, hs] = out.astype(BF16)


def _gdn_call(dn, z, gb4, conv_w, out_norm, tt):
    b, t, _ = dn.shape
    nc = tt // DN_CHUNK
    return pl.pallas_call(
        functools.partial(_gdn_kernel, tt=tt),
        grid=(b, t // tt),
        in_specs=[
            pl.BlockSpec((1, tt, 3 * DN_WIDTH), lambda bi, i: (bi, i, 0)),
            pl.BlockSpec((1, tt, DN_WIDTH), lambda bi, i: (bi, i, 0)),
            pl.BlockSpec((1, 2 * DN_HEADS, nc, DN_CHUNK), lambda bi, i: (bi, 0, i, 0)),
            pl.BlockSpec(conv_w.shape, lambda bi, i: (0, 0)),
            pl.BlockSpec(out_norm.shape, lambda bi, i: (0, 0)),
        ],
        out_specs=pl.BlockSpec((1, tt, DN_WIDTH), lambda bi, i: (bi, i, 0)),
        out_shape=jax.ShapeDtypeStruct((b, t, DN_WIDTH), BF16),
        scratch_shapes=[
            pltpu.VMEM((tt + 8, 3 * DN_WIDTH), F32),
            pltpu.VMEM((tt, 3 * DN_WIDTH), F32),
            pltpu.VMEM((tt, DN_WIDTH), F32),
            pltpu.VMEM((tt, DN_WIDTH), F32),
            pltpu.VMEM((tt, DN_WIDTH), F32),
            pltpu.VMEM((tt, DN_WIDTH), F32),
            pltpu.VMEM((DN_HEADS, tt, DN_CHUNK), F32),
            pltpu.VMEM((DN_HEADS, DN_HEAD_DIM, DN_HEAD_DIM), F32),
        ],
        compiler_params=pltpu.CompilerParams(
            dimension_semantics=("parallel", "arbitrary"), vmem_limit_bytes=VMEM_LIMIT),
        name="gated_deltanet",
    )(dn, z, gb4, conv_w, out_norm)


def _post_kernel(x_ref, oa_ref, ob_ref, mod_ref, gain_ref, woa_ref, wob_ref, w1_ref, w2_ref, o_ref, *, ff_chunk):
    mix = _dot(oa_ref[0], woa_ref[...]) + _dot(ob_ref[0], wob_ref[...])
    x1 = x_ref[0] + mod_ref[2:3, :] * mix
    ms = jnp.mean(x1 * x1, axis=-1, keepdims=True)
    y = x1 * lax.rsqrt(ms + EPS) * gain_ref[...]
    hb = (y * (1.0 + mod_ref[4:5, :]) + mod_ref[3:4, :]).astype(BF16)
    d_ff = w1_ref.shape[1]
    acc = jnp.zeros(x1.shape, F32)
    for f in range(d_ff // ff_chunk):
        fs = slice(f * ff_chunk, (f + 1) * ff_chunk)
        hid = jnp.maximum(_dot(hb, w1_ref[:, fs]), 0.0)
        acc = acc + _dot((hid * hid).astype(BF16), w2_ref[fs, :])
    o_ref[0] = x1 + mod_ref[5:6, :] * acc


def _post_call(x, oa, ob, mod, l, gain, woa, wob, w1, w2, tm):
    b, t, d = x.shape
    full = lambda a: pl.BlockSpec(a.shape, lambda bi, i: (0,) * a.ndim)
    tok = lambda w: pl.BlockSpec((1, tm, w), lambda bi, i: (bi, i, 0))
    return pl.pallas_call(
        functools.partial(_post_kernel, ff_chunk=1024),
        grid=(b, t // tm),
        in_specs=[
            tok(d), tok(SB_WIDTH), tok(DN_WIDTH),
            pl.BlockSpec((None, None, N_MOD, d), lambda bi, i: (l, bi, 0, 0)),
            full(gain), full(woa), full(wob), full(w1), full(w2),
        ],
        out_specs=tok(d),
        out_shape=jax.ShapeDtypeStruct((b, t, d), F32),
        compiler_params=pltpu.CompilerParams(
            dimension_semantics=("parallel", "parallel"), vmem_limit_bytes=VMEM_LIMIT),
        name="outproj_mlp",
    )(x, oa, ob, mod, gain, woa, wob, w1, w2)


def kernel(x, c, w_ada, b_ada, norm_mix, norm_mlp, w_in, sb_q_norm, sb_k_norm, conv_w, a_log, dt_bias,
           dn_out_norm, w_out, w_ff1, w_ff2):
    b, t, d = x.shape
    depth = w_ada.shape[0]
    tm = 256
    tq = 256
    tt = 512

    mod = _ada_call(c, w_ada, b_ada).reshape(depth, b, N_MOD, d)

    hi = jnp.arange(SB_WIDTH) // SB_HEAD_DIM
    blk = jnp.where(hi[:, None] == hi[None, :], 1.0 / SB_HEAD_DIM, 0.0).astype(BF16)
    ti = jnp.arange(tm)
    cum = ((ti[:, None] // DN_CHUNK == ti[None, :] // DN_CHUNK) & (ti[:, None] <= ti[None, :])).astype(BF16)
    qi = jnp.arange(tq)
    u = (qi[:, None] > qi[None, :]).astype(BF16)
    pad4 = jnp.zeros((DN_HEADS,), F32)

    c1, c2, c3, c4 = SB_WIDTH * 3, SB_WIDTH * 3 + DN_WIDTH * 3, SB_WIDTH * 3 + DN_WIDTH * 4, w_in.shape[2]
    for l in range(depth):
        wl = w_in[l]
        wsb = wl[:, :c1].astype(BF16)
        wdn = wl[:, c1:c2].astype(BF16)
        wz = wl[:, c2:c3].astype(BF16)
        wab = jnp.pad(wl[:, c3:c4], ((0, 0), (0, AB_PAD - (c4 - c3)))).astype(BF16)
        qg = jnp.tile(sb_q_norm[l], SB_HEADS)[None, :]
        kg = jnp.tile(sb_k_norm[l], SB_HEADS)[None, :]
        alog8 = jnp.concatenate([a_log[l], pad4])[:, None]
        dtb8 = jnp.concatenate([dt_bias[l], pad4])[:, None]
        q, k, v, dn, z, gb = _premix_call(x, mod, l, norm_mix[l][None, :], wsb, wdn, wz, wab, qg, kg, blk,
                                          alog8, dtb8, cum, tm)
        oa = _sb_attn_call(q, k, v, u, tq)
        gb4 = gb.reshape(b, 2 * DN_HEADS, t // DN_CHUNK, DN_CHUNK)
        ob = _gdn_call(dn, z, gb4, conv_w[l], dn_out_norm[l][None, :], tt)
        x = _post_call(x, oa, ob, mod, l, norm_mlp[l][None, :],
                       w_out[l, :SB_WIDTH].astype(BF16), w_out[l, SB_WIDTH:].astype(BF16),
                       w_ff1[l].astype(BF16), w_ff2[l].astype(BF16), tm)
    return x
```

```python
import functools

import jax
import jax.numpy as jnp
from jax import lax
from jax.experimental import pallas as pl
from jax.experimental.pallas import tpu as pltpu

F32 = jnp.float32
BF16 = jnp.bfloat16
EPS = 1e-6

SB_HEADS = 8
SB_HEAD_DIM = 64
SB_WIDTH = SB_HEADS * SB_HEAD_DIM
DN_HEADS = 4
DN_HEAD_DIM = 128
DN_WIDTH = DN_HEADS * DN_HEAD_DIM
DN_CHUNK = 64
CONV_WIDTH = 4
N_MOD = 6
LANES = 128
AB_PAD = LANES
VMEM_LIMIT = 56 * 1024 * 1024
LOG2E = 1.4426950408889634
SB_SKIP_LOG2 = 100.0 * LOG2E

NN = (((1,), (0,)), ((), ()))
NT = (((1,), (1,)), ((), ()))
TN = (((0,), (0,)), ((), ()))


def _dot(a, b, dims=NN):
    return lax.dot_general(a, b, dims, preferred_element_type=F32)


def _split2(a):
    hi = a.astype(BF16)
    lo = (a - hi.astype(F32)).astype(BF16)
    return hi, lo


def _split3(a):
    hi = a.astype(BF16)
    r = a - hi.astype(F32)
    mid = r.astype(BF16)
    lo = (r - mid.astype(F32)).astype(BF16)
    return hi, mid, lo


def _dot_x3(a, b, dims=NN):
    ah, al = _split2(a)
    bh, bl = _split2(b)
    return _dot(ah, bh, dims) + (_dot(ah, bl, dims) + _dot(al, bh, dims))


def _dot_x1(a, b, dims=NN):
    return _dot(a.astype(BF16), b.astype(BF16), dims)


def _bmm(a, b, dims, dot):
    return jnp.stack([dot(a[g], b[g], dims) for g in range(a.shape[0])])


GDN_DOT_SCORES = _dot_x1
GDN_DOT_INVERSE = _dot_x1
GDN_DOT_SOLVE = _dot_x1
GDN_DOT_CHUNK = _dot_x1
GDN_DOT_STATE = _dot_x1
GDN_DOT_OUT = _dot_x1


def _sigmoid(x):
    return 1.0 / (1.0 + jnp.exp(-x))


def _softplus(x):
    return jnp.maximum(x, 0.0) + jnp.log1p(jnp.exp(-jnp.abs(x)))


def _ada_kernel(c_ref, w_ref, b_ref, o_ref):
    c = c_ref[...]
    cond = c * _sigmoid(c)
    o_ref[0] = _dot_x3(cond, w_ref[0]) + b_ref[0]


def _ada_call(c, w_ada, b_ada):
    depth, d, n = w_ada.shape
    b = c.shape[0]
    tn = 1536
    return pl.pallas_call(
        _ada_kernel,
        grid=(depth, n // tn),
        in_specs=[
            pl.BlockSpec((b, d), lambda l, j: (0, 0)),
            pl.BlockSpec((1, d, tn), lambda l, j: (l, 0, j)),
            pl.BlockSpec((1, 1, tn), lambda l, j: (l, 0, j)),
        ],
        out_specs=pl.BlockSpec((1, b, tn), lambda l, j: (l, 0, j)),
        out_shape=jax.ShapeDtypeStruct((depth, b, n), F32),
        compiler_params=pltpu.CompilerParams(
            dimension_semantics=("arbitrary", "arbitrary"), vmem_limit_bytes=VMEM_LIMIT),
        name="adaln_mod",
    )(c, w_ada, b_ada.reshape(depth, 1, n))


def _premix_kernel(x_ref, mod_ref, gain_ref, wsb_ref, wdn_ref, wz_ref, wab_ref, qg_ref, kg_ref,
                   blk_ref, alog_ref, dtb_ref, cum_ref,
                   q_ref, k_ref, v_ref, dn_ref, z_ref, gbc_ref, gbr_ref):
    x = x_ref[0]
    ms = jnp.mean(x * x, axis=-1, keepdims=True)
    y = x * lax.rsqrt(ms + EPS) * gain_ref[...]
    h = y * (1.0 + mod_ref[1:2, :]) + mod_ref[0:1, :]
    hb = h.astype(BF16)

    sb = _dot(hb, wsb_ref[...])
    blk = blk_ref[...]

    def head_rmsnorm(a, g):
        sh, sl = _split2(a * a)
        m = _dot(sh, blk) + _dot(sl, blk)
        return a * lax.rsqrt(m + EPS) * g

    q = head_rmsnorm(sb[:, :SB_WIDTH], qg_ref[...]) * (SB_HEAD_DIM ** -0.5 * LOG2E)
    k = head_rmsnorm(sb[:, SB_WIDTH:2 * SB_WIDTH], kg_ref[...])
    q_ref[0] = q.astype(BF16)
    k_ref[0] = k.astype(BF16)
    v_ref[0] = sb[:, 2 * SB_WIDTH:].astype(BF16)

    dn_ref[0] = _dot(hb, wdn_ref[...])
    z_ref[0] = _dot(hb, wz_ref[...])

    ab = _dot(hb, wab_ref[...])
    lane = lax.broadcasted_iota(jnp.int32, ab.shape, 1)
    g = -jnp.exp(alog_ref[...]) * _softplus(ab + dtb_ref[...])
    g1, g2, g3 = _split3(g)
    cum = cum_ref[...]
    gc = _dot(cum, g1) + (_dot(cum, g2) + _dot(cum, g3))
    gb = jnp.where(lane < DN_HEADS, gc, _sigmoid(ab))
    gbc_ref[0] = gb
    gbr_ref[0] = gb.T[0:2 * DN_HEADS, :]


def _premix_call(x, mod, l, gain, wsb, wdn, wz, wab, qg, kg, blk, alog, dtb, cum, tm):
    b, t, d = x.shape
    full = lambda a: pl.BlockSpec(a.shape, lambda bi, i: (0,) * a.ndim)
    tok = lambda w: pl.BlockSpec((1, tm, w), lambda bi, i: (bi, i, 0))
    return pl.pallas_call(
        _premix_kernel,
        grid=(b, t // tm),
        in_specs=[
            tok(d),
            pl.BlockSpec((None, None, N_MOD, d), lambda bi, i: (l, bi, 0, 0)),
            full(gain), full(wsb), full(wdn), full(wz), full(wab), full(qg), full(kg), full(blk),
            full(alog), full(dtb), full(cum),
        ],
        out_specs=[tok(SB_WIDTH), tok(SB_WIDTH), tok(SB_WIDTH), tok(3 * DN_WIDTH), tok(DN_WIDTH),
                   tok(AB_PAD),
                   pl.BlockSpec((1, 2 * DN_HEADS, tm), lambda bi, i: (bi, 0, i))],
        out_shape=[
            jax.ShapeDtypeStruct((b, t, SB_WIDTH), BF16),
            jax.ShapeDtypeStruct((b, t, SB_WIDTH), BF16),
            jax.ShapeDtypeStruct((b, t, SB_WIDTH), BF16),
            jax.ShapeDtypeStruct((b, t, 3 * DN_WIDTH), F32),
            jax.ShapeDtypeStruct((b, t, DN_WIDTH), F32),
            jax.ShapeDtypeStruct((b, t, AB_PAD), F32),
            jax.ShapeDtypeStruct((b, 2 * DN_HEADS, t), F32),
        ],
        compiler_params=pltpu.CompilerParams(
            dimension_semantics=("parallel", "parallel"), vmem_limit_bytes=VMEM_LIMIT),
        name="premix_proj",
    )(x, mod, gain, wsb, wdn, wz, wab, qg, kg, blk, alog, dtb, cum)


def _sb_block(qh, kb, vb, u, carry, mask):
    z = _dot(qh, kb, NT)
    sp = jnp.maximum(z, 0.0) + jnp.log(1.0 + jnp.exp2(-jnp.abs(z))) * LOG2E
    if mask is not None:
        sp_used = jnp.where(mask, sp, 0.0)
    else:
        sp_used = sp
    spb = sp_used.astype(BF16)
    suffix = _dot(spb, u)
    arg = (z - sp) - suffix
    if carry is not None:
        arg = arg - carry
    att = jnp.exp2(arg)
    if mask is not None:
        att = jnp.where(mask, att, 0.0)
    pv = _dot(att.astype(BF16), vb)
    total = suffix[:, 0:1] + spb[:, 0:1].astype(F32)
    return pv, total


def _sb_attn_kernel(q_ref, k_ref, v_ref, u_ref, o_ref, *, tq):
    i = pl.program_id(2)
    q = q_ref[0]
    lane = lax.broadcasted_iota(jnp.int32, (1, LANES), 1)
    first = lane < SB_HEAD_DIM
    zero = jnp.zeros_like(q)
    qs = (jnp.where(first, q, zero), jnp.where(first, zero, q))
    u = u_ref[...]
    row = lax.broadcasted_iota(jnp.int32, (tq, tq), 0)
    col = lax.broadcasted_iota(jnp.int32, (tq, tq), 1)
    causal = col < row

    def kv_block(j):
        start = pl.multiple_of(j * tq, tq)
        return k_ref[0, pl.ds(start, tq), :], v_ref[0, pl.ds(start, tq), :]

    kd, vd = kv_block(i)
    kp, vp = kv_block(jnp.maximum(i - 1, 0))
    has_prev = jnp.where(i > 0, 1.0, 0.0)
    accs, cs = [], []
    for hh in range(2):
        pv_d, tot_d = _sb_block(qs[hh], kd, vd, u, None, causal)
        pv_p, tot_p = _sb_block(qs[hh], kp, vp, u, tot_d, None)
        accs.append(pv_d + has_prev * pv_p)
        cs.append(tot_d + has_prev * tot_p)

    def cond(st):
        j, _, _, c0, c1 = st
        return jnp.logical_and(j >= 0, jnp.min(jnp.minimum(c0, c1)) < SB_SKIP_LOG2)

    def body(st):
        j, a0, a1, c0, c1 = st
        kb, vb = kv_block(j)
        pv0, t0 = _sb_block(qs[0], kb, vb, u, c0, None)
        pv1, t1 = _sb_block(qs[1], kb, vb, u, c1, None)
        return j - 1, a0 + pv0, a1 + pv1, c0 + t0, c1 + t1

    _, a0, a1, _, _ = lax.while_loop(cond, body, (i - 2, accs[0], accs[1], cs[0], cs[1]))
    o_ref[0] = jnp.where(first, a0, a1).astype(BF16)


def _sb_attn_call(q, k, v, u, tq):
    b, t, _ = q.shape
    return pl.pallas_call(
        functools.partial(_sb_attn_kernel, tq=tq),
        grid=(b, SB_WIDTH // LANES, t // tq),
        in_specs=[
            pl.BlockSpec((1, tq, LANES), lambda bi, p, i: (bi, i, p)),
            pl.BlockSpec((1, t, LANES), lambda bi, p, i: (bi, 0, p)),
            pl.BlockSpec((1, t, LANES), lambda bi, p, i: (bi, 0, p)),
            pl.BlockSpec((tq, tq), lambda bi, p, i: (0, 0)),
        ],
        out_specs=pl.BlockSpec((1, tq, LANES), lambda bi, p, i: (bi, i, p)),
        out_shape=jax.ShapeDtypeStruct((b, t, SB_WIDTH), BF16),
        compiler_params=pltpu.CompilerParams(
            dimension_semantics=("parallel", "parallel", "arbitrary"), vmem_limit_bytes=VMEM_LIMIT),
        name="sb_attention",
    )(q, k, v, u)


def _unit_lower_inverse(a, ri, ci):
    mm = lambda p_, q_: _bmm(p_, q_, NN, GDN_DOT_INVERSE)
    same = ri == ci
    x = None
    for level in range(1, 7):
        same_next = (ri >> level) == (ci >> level)
        off = jnp.where(jnp.logical_and(same_next, jnp.logical_not(same)), a, 0.0)
        if x is None:
            x = jnp.where(ri == ci, 1.0, 0.0) - off
        else:
            x = x - mm(x, mm(off, x))
        same = same_next
    return x


def _gdn_kernel(dn_ref, z_ref, gbc_ref, gbr_ref, cw_ref, on_ref, o_ref, xbuf, qkv_s, state, *, tt):
    c_sz = DN_CHUNK
    hd = DN_HEAD_DIM
    nc = tt // c_sz

    @pl.when(pl.program_id(1) == 0)
    def _():
        state[...] = jnp.zeros_like(state)
        xbuf[0:8, :] = jnp.zeros((8, 3 * DN_WIDTH), F32)

    xbuf[8:8 + tt, :] = dn_ref[0]
    for cb in range(3 * DN_HEADS):
        ls = slice(cb * hd, (cb + 1) * hd)
        acc = cw_ref[CONV_WIDTH - 1:CONV_WIDTH, ls] * xbuf[8:8 + tt, ls]
        for kk in range(CONV_WIDTH - 1):
            off = 8 - (CONV_WIDTH - 1) + kk
            acc = acc + cw_ref[kk:kk + 1, ls] * xbuf[off:off + tt, ls]
        act = acc * _sigmoid(acc)
        if cb < 2 * DN_HEADS:
            act = act * lax.rsqrt(jnp.sum(act * act, axis=-1, keepdims=True) + EPS)
        if cb < DN_HEADS:
            act = act * (hd ** -0.5)
        qkv_s[:, ls] = act
    xbuf[0:8, :] = xbuf[tt:tt + 8, :]

    ri = lax.broadcasted_iota(jnp.int32, (c_sz, c_sz), 0)
    ci = lax.broadcasted_iota(jnp.int32, (c_sz, c_sz), 1)
    tril_incl = ci <= ri
    tril_strict = ci < ri

    gcol = gbc_ref[0].reshape(nc, c_sz, AB_PAD)
    outs = []
    for h in range(DN_HEADS):
        q = qkv_s[:, h * hd:(h + 1) * hd].reshape(nc, c_sz, hd)
        k = qkv_s[:, DN_WIDTH + h * hd:DN_WIDTH + (h + 1) * hd].reshape(nc, c_sz, hd)
        v = qkv_s[:, 2 * DN_WIDTH + h * hd:2 * DN_WIDTH + (h + 1) * hd].reshape(nc, c_sz, hd)
        gc_col = gcol[:, :, h:h + 1]
        beta_col = gcol[:, :, DN_HEADS + h:DN_HEADS + h + 1]
        gc_row = gbr_ref[0, h]
        g_last = gc_row[:, :, c_sz - 1:c_sz]
        decay = jnp.where(tril_incl, jnp.exp(jnp.where(tril_incl, gc_col - gc_row, 0.0)), 0.0)
        k_beta = k * beta_col
        a = jnp.where(tril_strict, _bmm(k_beta, k, NT, GDN_DOT_SCORES) * decay, 0.0)
        t_mat = _unit_lower_inverse(a, ri, ci)
        rhs = jnp.concatenate([v * beta_col, k_beta * jnp.exp(gc_col)], axis=-1)
        uw = _bmm(t_mat, rhs, NN, GDN_DOT_SOLVE)
        intra = jnp.where(tril_incl, _bmm(q, k, NT, GDN_DOT_SCORES) * decay, 0.0)
        kd = k * jnp.exp(g_last - gc_col)
        iuw = _bmm(intra, uw, NN, GDN_DOT_CHUNK)
        kuw = _bmm(kd, uw, TN, GDN_DOT_CHUNK)
        q_eff = q * jnp.exp(gc_col) - iuw[:, :, hd:]
        o_base = iuw[:, :, :hd]
        e_last = jnp.exp(g_last)
        kw = kuw[:, :, hd:]
        n_mat = kuw[:, :, :hd]

        s = state[h]
        s_before = []
        for c in range(nc):
            s_before.append(s)
            s = (e_last[c] * s - GDN_DOT_STATE(kw[c], s)) + n_mat[c]
        state[h] = s
        o = _bmm(q_eff, jnp.stack(s_before), NN, GDN_DOT_OUT) + o_base
        o = o.reshape(tt, hd)
        ms = jnp.mean(o * o, axis=-1, keepdims=True)
        zz = z_ref[0, :, h * hd:(h + 1) * hd]
        outs.append((o * lax.rsqrt(ms + EPS) * on_ref[...] * (zz * _sigmoid(zz))).astype(BF16))
    o_ref[0] = jnp.concatenate(outs, axis=-1)


def _gdn_call(dn, z, gbc, gbr5, conv_w, out_norm, tt):
    b, t, _ = dn.shape
    nc = tt // DN_CHUNK
    return pl.pallas_call(
        functools.partial(_gdn_kernel, tt=tt),
        grid=(b, t // tt),
        in_specs=[
            pl.BlockSpec((1, tt, 3 * DN_WIDTH), lambda bi, i: (bi, i, 0)),
            pl.BlockSpec((1, tt, DN_WIDTH), lambda bi, i: (bi, i, 0)),
            pl.BlockSpec((1, tt, AB_PAD), lambda bi, i: (bi, i, 0)),
            pl.BlockSpec((1, 2 * DN_HEADS, nc, 1, DN_CHUNK), lambda bi, i: (bi, 0, i, 0, 0)),
            pl.BlockSpec(conv_w.shape, lambda bi, i: (0, 0)),
            pl.BlockSpec(out_norm.shape, lambda bi, i: (0, 0)),
        ],
        out_specs=pl.BlockSpec((1, tt, DN_WIDTH), lambda bi, i: (bi, i, 0)),
        out_shape=jax.ShapeDtypeStruct((b, t, DN_WIDTH), BF16),
        scratch_shapes=[
            pltpu.VMEM((tt + 8, 3 * DN_WIDTH), F32),
            pltpu.VMEM((tt, 3 * DN_WIDTH), F32),
            pltpu.VMEM((DN_HEADS, DN_HEAD_DIM, DN_HEAD_DIM), F32),
        ],
        compiler_params=pltpu.CompilerParams(
            dimension_semantics=("parallel", "arbitrary"), vmem_limit_bytes=VMEM_LIMIT),
        name="gated_deltanet",
    )(dn, z, gbc, gbr5, conv_w, out_norm)


def _post_kernel(x_ref, oa_ref, ob_ref, mod_ref, gain_ref, woa_ref, wob_ref, w1_ref, w2_ref, o_ref, *, ff_chunk):
    mix = _dot(oa_ref[0], woa_ref[...]) + _dot(ob_ref[0], wob_ref[...])
    x1 = x_ref[0] + mod_ref[2:3, :] * mix
    ms = jnp.mean(x1 * x1, axis=-1, keepdims=True)
    y = x1 * lax.rsqrt(ms + EPS) * gain_ref[...]
    hb = (y * (1.0 + mod_ref[4:5, :]) + mod_ref[3:4, :]).astype(BF16)
    d_ff = w1_ref.shape[1]
    acc = jnp.zeros(x1.shape, F32)
    for f in range(d_ff // ff_chunk):
        fs = slice(f * ff_chunk, (f + 1) * ff_chunk)
        hid = jnp.maximum(_dot(hb, w1_ref[:, fs]), 0.0)
        acc = acc + _dot((hid * hid).astype(BF16), w2_ref[fs, :])
    o_ref[0] = x1 + mod_ref[5:6, :] * acc


def _post_call(x, oa, ob, mod, l, gain, woa, wob, w1, w2, tm):
    b, t, d = x.shape
    full = lambda a: pl.BlockSpec(a.shape, lambda bi, i: (0,) * a.ndim)
    tok = lambda w: pl.BlockSpec((1, tm, w), lambda bi, i: (bi, i, 0))
    return pl.pallas_call(
        functools.partial(_post_kernel, ff_chunk=1024),
        grid=(b, t // tm),
        in_specs=[
            tok(d), tok(SB_WIDTH), tok(DN_WIDTH),
            pl.BlockSpec((None, None, N_MOD, d), lambda bi, i: (l, bi, 0, 0)),
            full(gain), full(woa), full(wob), full(w1), full(w2),
        ],
        out_specs=tok(d),
        out_shape=jax.ShapeDtypeStruct((b, t, d), F32),
        compiler_params=pltpu.CompilerParams(
            dimension_semantics=("parallel", "parallel"), vmem_limit_bytes=VMEM_LIMIT),
        name="outproj_mlp",
    )(x, oa, ob, mod, gain, woa, wob, w1, w2)


def kernel(x, c, w_ada, b_ada, norm_mix, norm_mlp, w_in, sb_q_norm, sb_k_norm, conv_w, a_log, dt_bias,
           dn_out_norm, w_out, w_ff1, w_ff2):
    b, t, d = x.shape
    depth = w_ada.shape[0]
    tm = 256
    tq = 256
    tt = 512

    mod = _ada_call(c, w_ada, b_ada).reshape(depth, b, N_MOD, d)

    hi = jnp.arange(SB_WIDTH) // SB_HEAD_DIM
    blk = jnp.where(hi[:, None] == hi[None, :], 1.0 / SB_HEAD_DIM, 0.0).astype(BF16)
    ti = jnp.arange(tm)
    cum = ((ti[:, None] // DN_CHUNK == ti[None, :] // DN_CHUNK) & (ti[None, :] <= ti[:, None])).astype(BF16)
    qi = jnp.arange(tq)
    u = (qi[:, None] > qi[None, :]).astype(BF16)
    pad = jnp.zeros((AB_PAD - DN_HEADS,), F32)

    c1, c2, c3, c4 = SB_WIDTH * 3, SB_WIDTH * 3 + DN_WIDTH * 3, SB_WIDTH * 3 + DN_WIDTH * 4, w_in.shape[2]
    for l in range(depth):
        wl = w_in[l]
        wsb = wl[:, :c1].astype(BF16)
        wdn = wl[:, c1:c2].astype(BF16)
        wz = wl[:, c2:c3].astype(BF16)
        wab = jnp.pad(wl[:, c3:c4], ((0, 0), (0, AB_PAD - (c4 - c3)))).astype(BF16)
        qg = jnp.tile(sb_q_norm[l], SB_HEADS)[None, :]
        kg = jnp.tile(sb_k_norm[l], SB_HEADS)[None, :]
        alog = jnp.concatenate([a_log[l], pad])[None, :]
        dtb = jnp.concatenate([dt_bias[l], pad])[None, :]
        q, k, v, dn, z, gbc, gbr = _premix_call(x, mod, l, norm_mix[l][None, :], wsb, wdn, wz, wab, qg, kg,
                                                blk, alog, dtb, cum, tm)
        oa = _sb_attn_call(q, k, v, u, tq)
        gbr5 = gbr.reshape(b, 2 * DN_HEADS, t // DN_CHUNK, 1, DN_CHUNK)
        ob = _gdn_call(dn, z, gbc, gbr5, conv_w[l], dn_out_norm[l][None, :], tt)
        x = _post_call(x, oa, ob, mod, l, norm_mlp[l][None, :],
                       w_out[l, :SB_WIDTH].astype(BF16), w_out[l, SB_WIDTH:].astype(BF16),
                       w_ff1[l].astype(BF16), w_ff2[l].astype(BF16), tm)
    return x
```

```python
import functools

import jax
import jax.numpy as jnp
from jax import lax
from jax.experimental import pallas as pl
from jax.experimental.pallas import tpu as pltpu

F32 = jnp.float32
BF16 = jnp.bfloat16
EPS = 1e-6

SB_HEADS = 8
SB_HEAD_DIM = 64
SB_WIDTH = SB_HEADS * SB_HEAD_DIM
DN_HEADS = 4
DN_HEAD_DIM = 128
DN_WIDTH = DN_HEADS * DN_HEAD_DIM
DN_CHUNK = 64
CONV_WIDTH = 4
N_MOD = 6
LANES = 128
AB_PAD = LANES
VMEM_LIMIT = 56 * 1024 * 1024
LOG2E = 1.4426950408889634
SB_SKIP_LOG2 = 100.0 * LOG2E

NN = (((1,), (0,)), ((), ()))
NT = (((1,), (1,)), ((), ()))
TN = (((0,), (0,)), ((), ()))


def _dot(a, b, dims=NN):
    return lax.dot_general(a, b, dims, preferred_element_type=F32)


def _split2(a):
    hi = a.astype(BF16)
    lo = (a - hi.astype(F32)).astype(BF16)
    return hi, lo


def _split3(a):
    hi = a.astype(BF16)
    r = a - hi.astype(F32)
    mid = r.astype(BF16)
    lo = (r - mid.astype(F32)).astype(BF16)
    return hi, mid, lo


def _dot_x3(a, b, dims=NN):
    ah, al = _split2(a)
    bh, bl = _split2(b)
    return _dot(ah, bh, dims) + (_dot(ah, bl, dims) + _dot(al, bh, dims))


def _dot_x1(a, b, dims=NN):
    return _dot(a.astype(BF16), b.astype(BF16), dims)


def _bmm(a, b, dims, dot):
    return jnp.stack([dot(a[g], b[g], dims) for g in range(a.shape[0])])


GDN_DOT_SCORES = _dot_x1
GDN_DOT_INVERSE = _dot_x1
GDN_DOT_SOLVE = _dot_x1
GDN_DOT_CHUNK = _dot_x1
GDN_DOT_STATE = _dot_x1
GDN_DOT_OUT = _dot_x1


def _sigmoid(x):
    return 1.0 / (1.0 + jnp.exp(-x))


def _softplus(x):
    return jnp.maximum(x, 0.0) + jnp.log1p(jnp.exp(-jnp.abs(x)))


def _ada_kernel(c_ref, w_ref, b_ref, o_ref):
    c = c_ref[...]
    cond = c * _sigmoid(c)
    o_ref[0] = _dot_x3(cond, w_ref[0]) + b_ref[0]


def _ada_call(c, w_ada, b_ada):
    depth, d, n = w_ada.shape
    b = c.shape[0]
    tn = 1536
    return pl.pallas_call(
        _ada_kernel,
        grid=(depth, n // tn),
        in_specs=[
            pl.BlockSpec((b, d), lambda l, j: (0, 0)),
            pl.BlockSpec((1, d, tn), lambda l, j: (l, 0, j)),
            pl.BlockSpec((1, 1, tn), lambda l, j: (l, 0, j)),
        ],
        out_specs=pl.BlockSpec((1, b, tn), lambda l, j: (l, 0, j)),
        out_shape=jax.ShapeDtypeStruct((depth, b, n), F32),
        compiler_params=pltpu.CompilerParams(
            dimension_semantics=("arbitrary", "arbitrary"), vmem_limit_bytes=VMEM_LIMIT),
        name="adaln_mod",
    )(c, w_ada, b_ada.reshape(depth, 1, n))


def _premix_kernel(x_ref, mod_ref, gain_ref, wsb_ref, wdn_ref, wz_ref, wab_ref, qg_ref, kg_ref,
                   blk_ref, alog_ref, dtb_ref, cum_ref,
                   q_ref, k_ref, v_ref, dn_ref, z_ref, gbc_ref, gbr_ref):
    x = x_ref[0]
    ms = jnp.mean(x * x, axis=-1, keepdims=True)
    y = x * lax.rsqrt(ms + EPS) * gain_ref[...]
    h = y * (1.0 + mod_ref[1:2, :]) + mod_ref[0:1, :]
    hb = h.astype(BF16)

    sb = _dot(hb, wsb_ref[...])
    blk = blk_ref[...]

    def head_rmsnorm(a, g):
        m = _dot((a * a).astype(BF16), blk)
        return a * lax.rsqrt(m + EPS) * g

    q = head_rmsnorm(sb[:, :SB_WIDTH], qg_ref[...]) * (SB_HEAD_DIM ** -0.5 * LOG2E)
    k = head_rmsnorm(sb[:, SB_WIDTH:2 * SB_WIDTH], kg_ref[...])
    q_ref[0] = q.astype(BF16)
    k_ref[0] = k.astype(BF16)
    v_ref[0] = sb[:, 2 * SB_WIDTH:].astype(BF16)

    dn_ref[0] = _dot(hb, wdn_ref[...])
    z_ref[0] = _dot(hb, wz_ref[...])

    ab = _dot(hb, wab_ref[...])
    lane = lax.broadcasted_iota(jnp.int32, ab.shape, 1)
    g = -jnp.exp(alog_ref[...]) * _softplus(ab + dtb_ref[...])
    cum = cum_ref[...]
    rows = cum.shape[0]
    gcs = []
    for r0 in range(0, g.shape[0], rows):
        g1, g2, g3 = _split3(g[r0:r0 + rows])
        gcs.append(_dot(cum, g1) + (_dot(cum, g2) + _dot(cum, g3)))
    gc = jnp.concatenate(gcs, axis=0)
    gb = jnp.where(lane < DN_HEADS, gc, _sigmoid(ab))
    gbc_ref[0] = gb
    gbr_ref[0] = gb.T[0:2 * DN_HEADS, :]


def _premix_call(x, mod, l, gain, wsb, wdn, wz, wab, qg, kg, blk, alog, dtb, cum, tm):
    b, t, d = x.shape
    full = lambda a: pl.BlockSpec(a.shape, lambda bi, i: (0,) * a.ndim, pipeline_mode=pl.Buffered(1))
    tok = lambda w: pl.BlockSpec((1, tm, w), lambda bi, i: (bi, i, 0))
    return pl.pallas_call(
        _premix_kernel,
        grid=(b, t // tm),
        in_specs=[
            tok(d),
            pl.BlockSpec((None, None, N_MOD, d), lambda bi, i: (l, bi, 0, 0)),
            full(gain), full(wsb), full(wdn), full(wz), full(wab), full(qg), full(kg), full(blk),
            full(alog), full(dtb), full(cum),
        ],
        out_specs=[tok(SB_WIDTH), tok(SB_WIDTH), tok(SB_WIDTH), tok(3 * DN_WIDTH), tok(DN_WIDTH),
                   tok(AB_PAD),
                   pl.BlockSpec((1, 2 * DN_HEADS, tm), lambda bi, i: (bi, 0, i))],
        out_shape=[
            jax.ShapeDtypeStruct((b, t, SB_WIDTH), BF16),
            jax.ShapeDtypeStruct((b, t, SB_WIDTH), BF16),
            jax.ShapeDtypeStruct((b, t, SB_WIDTH), BF16),
            jax.ShapeDtypeStruct((b, t, 3 * DN_WIDTH), F32),
            jax.ShapeDtypeStruct((b, t, DN_WIDTH), F32),
            jax.ShapeDtypeStruct((b, t, AB_PAD), F32),
            jax.ShapeDtypeStruct((b, 2 * DN_HEADS, t), F32),
        ],
        compiler_params=pltpu.CompilerParams(
            dimension_semantics=("parallel", "parallel"), vmem_limit_bytes=VMEM_LIMIT),
        name="premix_proj",
    )(x, mod, gain, wsb, wdn, wz, wab, qg, kg, blk, alog, dtb, cum)


SB_PAIRS = 2


def _sb_softplus2(z):
    return jnp.maximum(z, 0.0) + jnp.log(1.0 + jnp.exp2(-jnp.abs(z))) * LOG2E


def _sb_attn_kernel(q_ref, k_ref, v_ref, u_ref, o_ref, *, tq):
    i = pl.program_id(2)
    lane = lax.broadcasted_iota(jnp.int32, (1, LANES), 1)
    first = lane < SB_HEAD_DIM
    qs, lanes = [], []
    for p in range(SB_PAIRS):
        q = q_ref[0, :, p * LANES:(p + 1) * LANES]
        zero = jnp.zeros_like(q)
        qs += [jnp.where(first, q, zero), jnp.where(first, zero, q)]
        lanes += [slice(p * LANES, (p + 1) * LANES)] * 2
    n_heads = len(qs)
    u = u_ref[...]
    row = lax.broadcasted_iota(jnp.int32, (tq, tq), 0)
    col = lax.broadcasted_iota(jnp.int32, (tq, tq), 1)
    causal = col < row

    def kv_block(j, ls):
        start = pl.multiple_of(j * tq, tq)
        return k_ref[0, pl.ds(start, tq), ls], v_ref[0, pl.ds(start, tq), ls]

    has_prev = jnp.where(i > 0, 1.0, 0.0)
    jprev = jnp.maximum(i - 1, 0)
    tiles = [(hh, blk) for hh in range(n_heads) for blk in range(2)]
    kvs = [kv_block(i if blk == 0 else jprev, lanes[hh]) for hh, blk in tiles]
    zs = [_dot(qs[hh], kv[0], NT) for (hh, _), kv in zip(tiles, kvs)]
    sps = [_sb_softplus2(z) for z in zs]
    spbs = [(jnp.where(causal, sp, 0.0) if blk == 0 else sp).astype(BF16) for sp, (_, blk) in zip(sps, tiles)]
    incls = [_dot(spb, u) for spb in spbs]
    tots = [incl[:, 0:1] for incl in incls]
    atts = []
    for n, (hh, blk) in enumerate(tiles):
        if blk == 0:
            atts.append(jnp.where(causal, jnp.exp2(zs[n] - incls[n]), 0.0).astype(BF16))
        else:
            atts.append(jnp.exp2(zs[n] - incls[n] - tots[n - 1]).astype(BF16))
    pvs = [_dot(att, kv[1]) for att, kv in zip(atts, kvs)]
    accs = tuple(pvs[2 * hh] + has_prev * pvs[2 * hh + 1] for hh in range(n_heads))
    cs = tuple(tots[2 * hh] + has_prev * tots[2 * hh + 1] for hh in range(n_heads))

    def cond(st):
        j, _, c = st
        cmin = c[0]
        for cc in c[1:]:
            cmin = jnp.minimum(cmin, cc)
        return jnp.logical_and(j >= 0, jnp.min(cmin) < SB_SKIP_LOG2)

    def body(st):
        j, a, c = st
        new_a, new_c = [], []
        for hh in range(n_heads):
            kb, vb = kv_block(j, lanes[hh])
            z = _dot(qs[hh], kb, NT)
            incl = _dot(_sb_softplus2(z).astype(BF16), u)
            att = jnp.exp2(z - incl - c[hh]).astype(BF16)
            new_a.append(a[hh] + _dot(att, vb))
            new_c.append(c[hh] + incl[:, 0:1])
        return j - 1, tuple(new_a), tuple(new_c)

    _, accs, _ = lax.while_loop(cond, body, (i - 2, accs, cs))
    for p in range(SB_PAIRS):
        o_ref[0, :, p * LANES:(p + 1) * LANES] = jnp.where(first, accs[2 * p], accs[2 * p + 1]).astype(BF16)


def _sb_attn_call(q, k, v, u, tq):
    b, t, _ = q.shape
    w = SB_PAIRS * LANES
    return pl.pallas_call(
        functools.partial(_sb_attn_kernel, tq=tq),
        grid=(b, SB_WIDTH // w, t // tq),
        in_specs=[
            pl.BlockSpec((1, tq, w), lambda bi, p, i: (bi, i, p)),
            pl.BlockSpec((1, t, w), lambda bi, p, i: (bi, 0, p)),
            pl.BlockSpec((1, t, w), lambda bi, p, i: (bi, 0, p)),
            pl.BlockSpec((tq, tq), lambda bi, p, i: (0, 0)),
        ],
        out_specs=pl.BlockSpec((1, tq, w), lambda bi, p, i: (bi, i, p)),
        out_shape=jax.ShapeDtypeStruct((b, t, SB_WIDTH), BF16),
        compiler_params=pltpu.CompilerParams(
            dimension_semantics=("parallel", "parallel", "arbitrary"), vmem_limit_bytes=VMEM_LIMIT),
        name="sb_attention",
    )(q, k, v, u)


def _unit_lower_inverse(a, ri, ci):
    mm = lambda p_, q_: _bmm(p_, q_, NN, GDN_DOT_INVERSE)
    same = ri == ci
    x = None
    for level in range(1, 7):
        same_next = (ri >> level) == (ci >> level)
        off = jnp.where(jnp.logical_and(same_next, jnp.logical_not(same)), a, 0.0)
        if x is None:
            x = jnp.where(ri == ci, 1.0, 0.0) - off
        else:
            x = x - mm(x, mm(off, x))
        same = same_next
    return x


def _gdn_kernel(dn_ref, z_ref, gbc_ref, gbr_ref, cw_ref, on_ref, o_ref, xbuf, qkv_s, state, *, tt):
    c_sz = DN_CHUNK
    hd = DN_HEAD_DIM
    nc = tt // c_sz

    @pl.when(pl.program_id(1) == 0)
    def _():
        state[...] = jnp.zeros_like(state)
        xbuf[0:8, :] = jnp.zeros((8, 3 * DN_WIDTH), F32)

    xbuf[8:8 + tt, :] = dn_ref[0]
    for cb in range(3 * DN_HEADS):
        ls = slice(cb * hd, (cb + 1) * hd)
        acc = cw_ref[CONV_WIDTH - 1:CONV_WIDTH, ls] * xbuf[8:8 + tt, ls]
        for kk in range(CONV_WIDTH - 1):
            off = 8 - (CONV_WIDTH - 1) + kk
            acc = acc + cw_ref[kk:kk + 1, ls] * xbuf[off:off + tt, ls]
        act = acc * _sigmoid(acc)
        if cb < 2 * DN_HEADS:
            act = act * lax.rsqrt(jnp.sum(act * act, axis=-1, keepdims=True) + EPS)
        if cb < DN_HEADS:
            act = act * (hd ** -0.5)
        qkv_s[:, ls] = act
    xbuf[0:8, :] = xbuf[tt:tt + 8, :]

    ri = lax.broadcasted_iota(jnp.int32, (c_sz, c_sz), 0)
    ci = lax.broadcasted_iota(jnp.int32, (c_sz, c_sz), 1)
    tril_incl = ci <= ri
    tril_strict = ci < ri

    gcol = gbc_ref[0].reshape(nc, c_sz, AB_PAD)
    heads = range(DN_HEADS)

    def cat(f):
        return jnp.concatenate([f(h) for h in heads], axis=0)

    q = cat(lambda h: qkv_s[:, h * hd:(h + 1) * hd].reshape(nc, c_sz, hd))
    k = cat(lambda h: qkv_s[:, DN_WIDTH + h * hd:DN_WIDTH + (h + 1) * hd].reshape(nc, c_sz, hd))
    v = cat(lambda h: qkv_s[:, 2 * DN_WIDTH + h * hd:2 * DN_WIDTH + (h + 1) * hd].reshape(nc, c_sz, hd))
    gc_col = cat(lambda h: gcol[:, :, h:h + 1])
    beta_col = cat(lambda h: gcol[:, :, DN_HEADS + h:DN_HEADS + h + 1])
    gc_row = gbr_ref[0, 0:DN_HEADS].reshape(DN_HEADS * nc, 1, c_sz)
    g_last = gc_row[:, :, c_sz - 1:c_sz]
    decay = jnp.where(tril_incl, jnp.exp(jnp.where(tril_incl, gc_col - gc_row, 0.0)), 0.0)
    k_beta = k * beta_col
    a = jnp.where(tril_strict, _bmm(k_beta, k, NT, GDN_DOT_SCORES) * decay, 0.0)
    t_mat = _unit_lower_inverse(a, ri, ci)
    rhs = jnp.concatenate([v * beta_col, k_beta * jnp.exp(gc_col)], axis=-1)
    uw = _bmm(t_mat, rhs, NN, GDN_DOT_SOLVE)
    intra = jnp.where(tril_incl, _bmm(q, k, NT, GDN_DOT_SCORES) * decay, 0.0)
    kd = k * jnp.exp(g_last - gc_col)
    iuw = _bmm(intra, uw, NN, GDN_DOT_CHUNK)
    kuw = _bmm(kd, uw, TN, GDN_DOT_CHUNK)
    q_eff = q * jnp.exp(gc_col) - iuw[:, :, hd:]
    o_base = iuw[:, :, :hd]
    e_last = jnp.exp(g_last)
    kw = kuw[:, :, hd:]
    n_mat = kuw[:, :, :hd]

    s_cur = [state[h] for h in heads]
    s_before = [[None] * nc for _ in heads]
    for c in range(nc):
        for h in heads:
            g = h * nc + c
            s_before[h][c] = s_cur[h]
            s_cur[h] = (e_last[g] * s_cur[h] - GDN_DOT_STATE(kw[g], s_cur[h])) + n_mat[g]
    for h in heads:
        state[h] = s_cur[h]
    s_all = jnp.stack([s_before[h][c] for h in heads for c in range(nc)])
    o_all = _bmm(q_eff, s_all, NN, GDN_DOT_OUT) + o_base
    outs = []
    for h in heads:
        o = o_all[h * nc:(h + 1) * nc].reshape(tt, hd)
        ms = jnp.mean(o * o, axis=-1, keepdims=True)
        zz = z_ref[0, :, h * hd:(h + 1) * hd]
        outs.append((o * lax.rsqrt(ms + EPS) * on_ref[...] * (zz * _sigmoid(zz))).astype(BF16))
    o_ref[0] = jnp.concatenate(outs, axis=-1)


def _gdn_call(dn, z, gbc, gbr5, conv_w, out_norm, tt):
    b, t, _ = dn.shape
    nc = tt // DN_CHUNK
    return pl.pallas_call(
        functools.partial(_gdn_kernel, tt=tt),
        grid=(b, t // tt),
        in_specs=[
            pl.BlockSpec((1, tt, 3 * DN_WIDTH), lambda bi, i: (bi, i, 0)),
            pl.BlockSpec((1, tt, DN_WIDTH), lambda bi, i: (bi, i, 0)),
            pl.BlockSpec((1, tt, AB_PAD), lambda bi, i: (bi, i, 0)),
            pl.BlockSpec((1, 2 * DN_HEADS, nc, 1, DN_CHUNK), lambda bi, i: (bi, 0, i, 0, 0)),
            pl.BlockSpec(conv_w.shape, lambda bi, i: (0, 0)),
            pl.BlockSpec(out_norm.shape, lambda bi, i: (0, 0)),
        ],
        out_specs=pl.BlockSpec((1, tt, DN_WIDTH), lambda bi, i: (bi, i, 0)),
        out_shape=jax.ShapeDtypeStruct((b, t, DN_WIDTH), BF16),
        scratch_shapes=[
            pltpu.VMEM((tt + 8, 3 * DN_WIDTH), F32),
            pltpu.VMEM((tt, 3 * DN_WIDTH), F32),
            pltpu.VMEM((DN_HEADS, DN_HEAD_DIM, DN_HEAD_DIM), F32),
        ],
        compiler_params=pltpu.CompilerParams(
            dimension_semantics=("parallel", "arbitrary"), vmem_limit_bytes=VMEM_LIMIT),
        name="gated_deltanet",
    )(dn, z, gbc, gbr5, conv_w, out_norm)


def _post_kernel(x_ref, oa_ref, ob_ref, mod_ref, gain_ref, woa_ref, wob_ref, w1_ref, w2_ref, o_ref, *, ff_chunk):
    mix = _dot(oa_ref[0], woa_ref[...]) + _dot(ob_ref[0], wob_ref[...])
    x1 = x_ref[0] + mod_ref[2:3, :] * mix
    ms = jnp.mean(x1 * x1, axis=-1, keepdims=True)
    y = x1 * lax.rsqrt(ms + EPS) * gain_ref[...]
    hb = (y * (1.0 + mod_ref[4:5, :]) + mod_ref[3:4, :]).astype(BF16)
    d_ff = w1_ref.shape[1]
    acc = jnp.zeros(x1.shape, F32)
    for f in range(d_ff // ff_chunk):
        fs = slice(f * ff_chunk, (f + 1) * ff_chunk)
        hid = jnp.maximum(_dot(hb, w1_ref[:, fs]), 0.0)
        acc = acc + _dot((hid * hid).astype(BF16), w2_ref[fs, :])
    o_ref[0] = x1 + mod_ref[5:6, :] * acc


def _post_call(x, oa, ob, mod, l, gain, woa, wob, w1, w2, tm):
    b, t, d = x.shape
    full = lambda a: pl.BlockSpec(a.shape, lambda bi, i: (0,) * a.ndim, pipeline_mode=pl.Buffered(1))
    tok = lambda w: pl.BlockSpec((1, tm, w), lambda bi, i: (bi, i, 0))
    return pl.pallas_call(
        functools.partial(_post_kernel, ff_chunk=1024),
        grid=(b, t // tm),
        in_specs=[
            tok(d), tok(SB_WIDTH), tok(DN_WIDTH),
            pl.BlockSpec((None, None, N_MOD, d), lambda bi, i: (l, bi, 0, 0)),
            full(gain), full(woa), full(wob), full(w1), full(w2),
        ],
        out_specs=tok(d),
        out_shape=jax.ShapeDtypeStruct((b, t, d), F32),
        compiler_params=pltpu.CompilerParams(
            dimension_semantics=("parallel", "parallel"), vmem_limit_bytes=VMEM_LIMIT),
        name="outproj_mlp",
    )(x, oa, ob, mod, gain, woa, wob, w1, w2)


def kernel(x, c, w_ada, b_ada, norm_mix, norm_mlp, w_in, sb_q_norm, sb_k_norm, conv_w, a_log, dt_bias,
           dn_out_norm, w_out, w_ff1, w_ff2):
    b, t, d = x.shape
    depth = w_ada.shape[0]
    tm = 512
    tq = 256
    tt = 512
    cum_rows = 256

    mod = _ada_call(c, w_ada, b_ada).reshape(depth, b, N_MOD, d)

    hi = jnp.arange(SB_WIDTH) // SB_HEAD_DIM
    blk = jnp.where(hi[:, None] == hi[None, :], 1.0 / SB_HEAD_DIM, 0.0).astype(BF16)
    ti = jnp.arange(cum_rows)
    cum = ((ti[:, None] // DN_CHUNK == ti[None, :] // DN_CHUNK) & (ti[None, :] <= ti[:, None])).astype(BF16)
    qi = jnp.arange(tq)
    u = (qi[:, None] >= qi[None, :]).astype(BF16)
    pad = jnp.zeros((AB_PAD - DN_HEADS,), F32)

    c1, c2, c3, c4 = SB_WIDTH * 3, SB_WIDTH * 3 + DN_WIDTH * 3, SB_WIDTH * 3 + DN_WIDTH * 4, w_in.shape[2]
    for l in range(depth):
        wl = w_in[l]
        wsb = wl[:, :c1].astype(BF16)
        wdn = wl[:, c1:c2].astype(BF16)
        wz = wl[:, c2:c3].astype(BF16)
        wab = jnp.pad(wl[:, c3:c4], ((0, 0), (0, AB_PAD - (c4 - c3)))).astype(BF16)
        qg = jnp.tile(sb_q_norm[l], SB_HEADS)[None, :]
        kg = jnp.tile(sb_k_norm[l], SB_HEADS)[None, :]
        alog = jnp.concatenate([a_log[l], pad])[None, :]
        dtb = jnp.concatenate([dt_bias[l], pad])[None, :]
        q, k, v, dn, z, gbc, gbr = _premix_call(x, mod, l, norm_mix[l][None, :], wsb, wdn, wz, wab, qg, kg,
                                                blk, alog, dtb, cum, tm)
        oa = _sb_attn_call(q, k, v, u, tq)
        gbr5 = gbr.reshape(b, 2 * DN_HEADS, t // DN_CHUNK, 1, DN_CHUNK)
        ob = _gdn_call(dn, z, gbc, gbr5, conv_w[l], dn_out_norm[l][None, :], tt)
        x = _post_call(x, oa, ob, mod, l, norm_mlp[l][None, :],
                       w_out[l, :SB_WIDTH].astype(BF16), w_out[l, SB_WIDTH:].astype(BF16),
                       w_ff1[l].astype(BF16), w_ff2[l].astype(BF16), tm)
    return x
```

```python
import functools

import jax
import jax.numpy as jnp
from jax import lax
from jax.experimental import pallas as pl
from jax.experimental.pallas import tpu as pltpu

F32 = jnp.float32
BF16 = jnp.bfloat16
EPS = 1e-6

SB_HEADS = 8
SB_HEAD_DIM = 64
SB_WIDTH = SB_HEADS * SB_HEAD_DIM
DN_HEADS = 4
DN_HEAD_DIM = 128
DN_WIDTH = DN_HEADS * DN_HEAD_DIM
DN_CHUNK = 64
CONV_WIDTH = 4
N_MOD = 6
LANES = 128
AB_PAD = LANES
VMEM_LIMIT = 56 * 1024 * 1024
LOG2E = 1.4426950408889634
SB_SKIP_LOG2 = 100.0 * LOG2E

NN = (((1,), (0,)), ((), ()))
NT = (((1,), (1,)), ((), ()))
TN = (((0,), (0,)), ((), ()))


def _dot(a, b, dims=NN):
    return lax.dot_general(a, b, dims, preferred_element_type=F32)


def _split2(a):
    hi = a.astype(BF16)
    lo = (a - hi.astype(F32)).astype(BF16)
    return hi, lo


def _split3(a):
    hi = a.astype(BF16)
    r = a - hi.astype(F32)
    mid = r.astype(BF16)
    lo = (r - mid.astype(F32)).astype(BF16)
    return hi, mid, lo


def _dot_x3(a, b, dims=NN):
    ah, al = _split2(a)
    bh, bl = _split2(b)
    return _dot(ah, bh, dims) + (_dot(ah, bl, dims) + _dot(al, bh, dims))


def _dot_x1(a, b, dims=NN):
    return _dot(a.astype(BF16), b.astype(BF16), dims)


def _bmm(a, b, dims, dot):
    return jnp.stack([dot(a[g], b[g], dims) for g in range(a.shape[0])])


GDN_DOT_SCORES = _dot_x1
GDN_DOT_INVERSE = _dot_x1
GDN_DOT_SOLVE = _dot_x1
GDN_DOT_CHUNK = _dot_x1
GDN_DOT_STATE = _dot_x1
GDN_DOT_OUT = _dot_x1


def _sigmoid(x):
    return 1.0 / (1.0 + jnp.exp(-x))


def _softplus(x):
    return jnp.maximum(x, 0.0) + jnp.log1p(jnp.exp(-jnp.abs(x)))


def _ada_kernel(c_ref, w_ref, b_ref, o_ref):
    c = c_ref[...]
    cond = c * _sigmoid(c)
    o_ref[0] = _dot_x3(cond, w_ref[0]) + b_ref[0]


def _ada_call(c, w_ada, b_ada):
    depth, d, n = w_ada.shape
    b = c.shape[0]
    tn = 1536
    return pl.pallas_call(
        _ada_kernel,
        grid=(depth, n // tn),
        in_specs=[
            pl.BlockSpec((b, d), lambda l, j: (0, 0)),
            pl.BlockSpec((1, d, tn), lambda l, j: (l, 0, j)),
            pl.BlockSpec((1, 1, tn), lambda l, j: (l, 0, j)),
        ],
        out_specs=pl.BlockSpec((1, b, tn), lambda l, j: (l, 0, j)),
        out_shape=jax.ShapeDtypeStruct((depth, b, n), F32),
        compiler_params=pltpu.CompilerParams(
            dimension_semantics=("arbitrary", "arbitrary"), vmem_limit_bytes=VMEM_LIMIT),
        name="adaln_mod",
    )(c, w_ada, b_ada.reshape(depth, 1, n))


def _premix_kernel(x_ref, mod_ref, gain_ref, wsb_ref, wdn_ref, wz_ref, wab_ref, qg_ref, kg_ref,
                   blk_ref, alog_ref, dtb_ref, cum_ref, cw_ref,
                   q_ref, k_ref, v_ref, dn_ref, z_ref, gbc_ref, gbr_ref, xbuf):
    tm = x_ref.shape[1]
    x = x_ref[0]
    ms = jnp.mean(x * x, axis=-1, keepdims=True)
    y = x * lax.rsqrt(ms + EPS) * gain_ref[...]
    h = y * (1.0 + mod_ref[1:2, :]) + mod_ref[0:1, :]
    hb = h.astype(BF16)

    @pl.when(pl.program_id(1) == 0)
    def _():
        xbuf[0:8, :] = jnp.zeros((8, 3 * DN_WIDTH), F32)

    hd = DN_HEAD_DIM
    gw = 2 * hd
    n_groups = 3 * DN_WIDTH // gw

    def dn_group(j):
        xbuf[8:8 + tm, j * gw:(j + 1) * gw] = _dot(hb, wdn_ref[:, j * gw:(j + 1) * gw])

    dn_group(0)
    sb_groups = []
    for j in range(n_groups):
        if j + 1 < n_groups:
            dn_group(j + 1)
        sb_groups.append(_dot(hb, wsb_ref[:, j * gw:(j + 1) * gw]))
        for cb in (2 * j, 2 * j + 1):
            ls = slice(cb * hd, (cb + 1) * hd)
            acc = cw_ref[CONV_WIDTH - 1:CONV_WIDTH, ls] * xbuf[8:8 + tm, ls]
            for kk in range(CONV_WIDTH - 1):
                off = 8 - (CONV_WIDTH - 1) + kk
                acc = acc + cw_ref[kk:kk + 1, ls] * xbuf[off:off + tm, ls]
            act = acc * _sigmoid(acc)
            if cb < 2 * DN_HEADS:
                act = act * lax.rsqrt(jnp.sum(act * act, axis=-1, keepdims=True) + EPS)
            if cb < DN_HEADS:
                act = act * (hd ** -0.5)
            dn_ref[0, :, ls] = act
    xbuf[0:8, :] = xbuf[tm:tm + 8, :]
    sb = jnp.concatenate(sb_groups, axis=-1)

    z_ref[0] = _dot(hb, wz_ref[...])
    ab = _dot(hb, wab_ref[...])

    blk = blk_ref[...]

    def head_rmsnorm(a, g):
        m = _dot((a * a).astype(BF16), blk)
        return a * lax.rsqrt(m + EPS) * g

    q = head_rmsnorm(sb[:, :SB_WIDTH], qg_ref[...]) * (SB_HEAD_DIM ** -0.5 * LOG2E)
    k = head_rmsnorm(sb[:, SB_WIDTH:2 * SB_WIDTH], kg_ref[...])
    q_ref[0] = q.astype(BF16)
    k_ref[0] = k.astype(BF16)
    v_ref[0] = sb[:, 2 * SB_WIDTH:].astype(BF16)

    lane =lax.broadcasted_iota(jnp.int32, ab.shape, 1)
    g = -jnp.exp(alog_ref[...]) * _softplus(ab + dtb_ref[...])
    cum = cum_ref[...]
    rows = cum.shape[0]
    gcs = []
    for r0 in range(0, g.shape[0], rows):
        g1, g2, g3 = _split3(g[r0:r0 + rows])
        gcs.append(_dot(cum, g1) + (_dot(cum, g2) + _dot(cum, g3)))
    gc = jnp.concatenate(gcs, axis=0)
    gb = jnp.where(lane < DN_HEADS, gc, _sigmoid(ab))
    gbc_ref[0] = gb
    gbr_ref[0] = gb.T[0:2 * DN_HEADS, :]


def _premix_call(x, mod, l, gain, wsb, wdn, wz, wab, qg, kg, blk, alog, dtb, cum, conv_w, tm):
    b, t, d = x.shape
    full = lambda a: pl.BlockSpec(a.shape, lambda bi, i: (0,) * a.ndim, pipeline_mode=pl.Buffered(1))
    tok = lambda w: pl.BlockSpec((1, tm, w), lambda bi, i: (bi, i, 0))
    return pl.pallas_call(
        _premix_kernel,
        grid=(b, t // tm),
        in_specs=[
            tok(d),
            pl.BlockSpec((None, None, N_MOD, d), lambda bi, i: (l, bi, 0, 0)),
            full(gain), full(wsb), full(wdn), full(wz), full(wab), full(qg), full(kg), full(blk),
            full(alog), full(dtb), full(cum), full(conv_w),
        ],
        out_specs=[tok(SB_WIDTH), tok(SB_WIDTH), tok(SB_WIDTH), tok(3 * DN_WIDTH), tok(DN_WIDTH),
                   tok(AB_PAD),
                   pl.BlockSpec((1, 2 * DN_HEADS, tm), lambda bi, i: (bi, 0, i))],
        out_shape=[
            jax.ShapeDtypeStruct((b, t, SB_WIDTH), BF16),
            jax.ShapeDtypeStruct((b, t, SB_WIDTH), BF16),
            jax.ShapeDtypeStruct((b, t, SB_WIDTH), BF16),
            jax.ShapeDtypeStruct((b, t, 3 * DN_WIDTH), F32),
            jax.ShapeDtypeStruct((b, t, DN_WIDTH), F32),
            jax.ShapeDtypeStruct((b, t, AB_PAD), F32),
            jax.ShapeDtypeStruct((b, 2 * DN_HEADS, t), F32),
        ],
        scratch_shapes=[pltpu.VMEM((tm + 8, 3 * DN_WIDTH), F32)],
        compiler_params=pltpu.CompilerParams(
            dimension_semantics=("parallel", "arbitrary"), vmem_limit_bytes=VMEM_LIMIT),
        name="premix_proj",
    )(x, mod, gain, wsb, wdn, wz, wab, qg, kg, blk, alog, dtb, cum, conv_w)


SB_PAIRS = 2


def _sb_softplus2(z):
    return jnp.maximum(z, 0.0) + jnp.log(1.0 + jnp.exp2(-jnp.abs(z))) * LOG2E


def _sb_attn_kernel(q_ref, k_ref, v_ref, u_ref, o_ref, *, tq):
    i = pl.program_id(2)
    lane = lax.broadcasted_iota(jnp.int32, (1, LANES), 1)
    first = lane < SB_HEAD_DIM
    qs, lanes = [], []
    for p in range(SB_PAIRS):
        q = q_ref[0, :, p * LANES:(p + 1) * LANES]
        zero = jnp.zeros_like(q)
        qs += [jnp.where(first, q, zero), jnp.where(first, zero, q)]
        lanes += [slice(p * LANES, (p + 1) * LANES)] * 2
    n_heads = len(qs)
    u = u_ref[...]
    row = lax.broadcasted_iota(jnp.int32, (tq, tq), 0)
    col = lax.broadcasted_iota(jnp.int32, (tq, tq), 1)
    causal = col < row

    def kv_block(j, ls):
        start = pl.multiple_of(j * tq, tq)
        return k_ref[0, pl.ds(start, tq), ls], v_ref[0, pl.ds(start, tq), ls]

    has_prev = jnp.where(i > 0, 1.0, 0.0)
    jprev = jnp.maximum(i - 1, 0)
    tiles = [(hh, blk) for hh in range(n_heads) for blk in range(2)]
    kvs = [kv_block(i if blk == 0 else jprev, lanes[hh]) for hh, blk in tiles]
    zs = [_dot(qs[hh], kv[0], NT) for (hh, _), kv in zip(tiles, kvs)]
    sps = [_sb_softplus2(z) for z in zs]
    spbs = [(jnp.where(causal, sp, 0.0) if blk == 0 else sp).astype(BF16) for sp, (_, blk) in zip(sps, tiles)]
    incls = [_dot(spb, u) for spb in spbs]
    tots = [incl[:, 0:1] for incl in incls]
    atts = []
    for n, (hh, blk) in enumerate(tiles):
        if blk == 0:
            atts.append(jnp.where(causal, jnp.exp2(zs[n] - incls[n]), 0.0).astype(BF16))
        else:
            atts.append(jnp.exp2(zs[n] - incls[n] - tots[n - 1]).astype(BF16))
    pvs = [_dot(att, kv[1]) for att, kv in zip(atts, kvs)]
    accs = tuple(pvs[2 * hh] + has_prev * pvs[2 * hh + 1] for hh in range(n_heads))
    cs = tuple(tots[2 * hh] + has_prev * tots[2 * hh + 1] for hh in range(n_heads))

    def cond(st):
        j, _, c = st
        cmin = c[0]
        for cc in c[1:]:
            cmin = jnp.minimum(cmin, cc)
        return jnp.logical_and(j >= 0, jnp.min(cmin) < SB_SKIP_LOG2)

    def body(st):
        j, a, c = st
        new_a, new_c = [], []
        for hh in range(n_heads):
            kb, vb = kv_block(j, lanes[hh])
            z = _dot(qs[hh], kb, NT)
            incl = _dot(_sb_softplus2(z).astype(BF16), u)
            att = jnp.exp2(z - incl - c[hh]).astype(BF16)
            new_a.append(a[hh] + _dot(att, vb))
            new_c.append(c[hh] + incl[:, 0:1])
        return j - 1, tuple(new_a), tuple(new_c)

    _, accs, _ = lax.while_loop(cond, body, (i - 2, accs, cs))
    for p in range(SB_PAIRS):
        o_ref[0, :, p * LANES:(p + 1) * LANES] = jnp.where(first, accs[2 * p], accs[2 * p + 1]).astype(BF16)


def _sb_attn_call(q, k, v, u, tq):
    b, t, _ = q.shape
    w = SB_PAIRS * LANES
    return pl.pallas_call(
        functools.partial(_sb_attn_kernel, tq=tq),
        grid=(b, SB_WIDTH // w, t // tq),
        in_specs=[
            pl.BlockSpec((1, tq, w), lambda bi, p, i: (bi, i, p)),
            pl.BlockSpec((1, t, w), lambda bi, p, i: (bi, 0, p)),
            pl.BlockSpec((1, t, w), lambda bi, p, i: (bi, 0, p)),
            pl.BlockSpec((tq, tq), lambda bi, p, i: (0, 0)),
        ],
        out_specs=pl.BlockSpec((1, tq, w), lambda bi, p, i: (bi, i, p)),
        out_shape=jax.ShapeDtypeStruct((b, t, SB_WIDTH), BF16),
        compiler_params=pltpu.CompilerParams(
            dimension_semantics=("parallel", "parallel", "arbitrary"), vmem_limit_bytes=VMEM_LIMIT),
        name="sb_attention",
    )(q, k, v, u)


def _unit_lower_inverse(a, ri, ci):
    mm = lambda p_, q_: _bmm(p_, q_, NN, GDN_DOT_INVERSE)
    same = ri == ci
    x = None
    for level in range(1, 7):
        same_next = (ri >> level) == (ci >> level)
        off = jnp.where(jnp.logical_and(same_next, jnp.logical_not(same)), a, 0.0)
        if x is None:
            x = jnp.where(ri == ci, 1.0, 0.0) - off
        else:
            x = x - mm(x, mm(off, x))
        same = same_next
    return x


def _gdn_kernel(qkv_ref, z_ref, gbc_ref, gbr_ref, on_ref, o_ref, state, *, tt):
    c_sz = DN_CHUNK
    hd = DN_HEAD_DIM
    nc = tt // c_sz
    qkv_s = qkv_ref.at[0]

    @pl.when(pl.program_id(1) == 0)
    def _():
        state[...] = jnp.zeros_like(state)

    ri = lax.broadcasted_iota(jnp.int32, (c_sz, c_sz), 0)
    ci = lax.broadcasted_iota(jnp.int32, (c_sz, c_sz), 1)
    tril_incl = ci <= ri
    tril_strict = ci < ri

    gcol = gbc_ref[0].reshape(nc, c_sz, AB_PAD)
    heads = range(DN_HEADS)

    def cat(f):
        return jnp.concatenate([f(h) for h in heads], axis=0)

    q = cat(lambda h: qkv_s[:, h * hd:(h + 1) * hd].reshape(nc, c_sz, hd))
    k = cat(lambda h: qkv_s[:, DN_WIDTH + h * hd:DN_WIDTH + (h + 1) * hd].reshape(nc, c_sz, hd))
    v = cat(lambda h: qkv_s[:, 2 * DN_WIDTH + h * hd:2 * DN_WIDTH + (h + 1) * hd].reshape(nc, c_sz, hd))
    gc_col = cat(lambda h: gcol[:, :, h:h + 1])
    beta_col = cat(lambda h: gcol[:, :, DN_HEADS + h:DN_HEADS + h + 1])
    gc_row = gbr_ref[0, 0:DN_HEADS].reshape(DN_HEADS * nc, 1, c_sz)
    g_last = gc_row[:, :, c_sz - 1:c_sz]
    decay = jnp.where(tril_incl, jnp.exp(jnp.where(tril_incl, gc_col - gc_row, 0.0)), 0.0)
    k_beta = k * beta_col
    a = jnp.where(tril_strict, _bmm(k_beta, k, NT, GDN_DOT_SCORES) * decay, 0.0)
    t_mat = _unit_lower_inverse(a, ri, ci)
    rhs = jnp.concatenate([v * beta_col, k_beta * jnp.exp(gc_col)], axis=-1)
    uw = _bmm(t_mat, rhs, NN, GDN_DOT_SOLVE)
    intra = jnp.where(tril_incl, _bmm(q, k, NT, GDN_DOT_SCORES) * decay, 0.0)
    kd = k * jnp.exp(g_last - gc_col)
    iuw = _bmm(intra, uw, NN, GDN_DOT_CHUNK)
    kuw = _bmm(kd, uw, TN, GDN_DOT_CHUNK)
    q_eff = q * jnp.exp(gc_col) - iuw[:, :, hd:]
    o_base = iuw[:, :, :hd]
    e_last = jnp.exp(g_last)
    kw = kuw[:, :, hd:]
    n_mat = kuw[:, :, :hd]

    s_cur = [state[h] for h in heads]
    s_before = [[None] * nc for _ in heads]
    for c in range(nc):
        for h in heads:
            g = h * nc + c
            s_before[h][c] = s_cur[h]
            s_cur[h] = (e_last[g] * s_cur[h] - GDN_DOT_STATE(kw[g], s_cur[h])) + n_mat[g]
    for h in heads:
        state[h] = s_cur[h]
    s_all = jnp.stack([s_before[h][c] for h in heads for c in range(nc)])
    o_all = _bmm(q_eff, s_all, NN, GDN_DOT_OUT) + o_base
    outs = []
    for h in heads:
        o = o_all[h * nc:(h + 1) * nc].reshape(tt, hd)
        ms = jnp.mean(o * o, axis=-1, keepdims=True)
        zz = z_ref[0, :, h * hd:(h + 1) * hd]
        outs.append((o * lax.rsqrt(ms + EPS) * on_ref[...] * (zz * _sigmoid(zz))).astype(BF16))
    o_ref[0] = jnp.concatenate(outs, axis=-1)


def _gdn_call(dn, z, gbc, gbr5, out_norm, tt):
    b, t, _ = dn.shape
    nc = tt // DN_CHUNK
    return pl.pallas_call(
        functools.partial(_gdn_kernel, tt=tt),
        grid=(b, t // tt),
        in_specs=[
            pl.BlockSpec((1, tt, 3 * DN_WIDTH), lambda bi, i: (bi, i, 0)),
            pl.BlockSpec((1, tt, DN_WIDTH), lambda bi, i: (bi, i, 0)),
            pl.BlockSpec((1, tt, AB_PAD), lambda bi, i: (bi, i, 0)),
            pl.BlockSpec((1, 2 * DN_HEADS, nc, 1, DN_CHUNK), lambda bi, i: (bi, 0, i, 0, 0)),
            pl.BlockSpec(out_norm.shape, lambda bi, i: (0, 0)),
        ],
        out_specs=pl.BlockSpec((1, tt, DN_WIDTH), lambda bi, i: (bi, i, 0)),
        out_shape=jax.ShapeDtypeStruct((b, t, DN_WIDTH), BF16),
        scratch_shapes=[
            pltpu.VMEM((DN_HEADS, DN_HEAD_DIM, DN_HEAD_DIM), F32),
        ],
        compiler_params=pltpu.CompilerParams(
            dimension_semantics=("parallel", "arbitrary"), vmem_limit_bytes=VMEM_LIMIT),
        name="gated_deltanet",
    )(dn, z, gbc, gbr5, out_norm)


def _post_kernel(x_ref, oa_ref, ob_ref, mod_ref, gain_ref, woa_ref, wob_ref, w1_ref, w2_ref, o_ref, *, ff_chunk):
    mix = _dot(oa_ref[0], woa_ref[...]) + _dot(ob_ref[0], wob_ref[...])
    x1 = x_ref[0] + mod_ref[2:3, :] * mix
    ms = jnp.mean(x1 * x1, axis=-1, keepdims=True)
    y = x1 * lax.rsqrt(ms + EPS) * gain_ref[...]
    hb = (y * (1.0 + mod_ref[4:5, :]) + mod_ref[3:4, :]).astype(BF16)
    d_ff = w1_ref.shape[1]
    acc = jnp.zeros(x1.shape, F32)
    for f in range(d_ff // ff_chunk):
        fs = slice(f * ff_chunk, (f + 1) * ff_chunk)
        hid = jnp.maximum(_dot(hb, w1_ref[:, fs]), 0.0)
        acc = acc + _dot((hid * hid).astype(BF16), w2_ref[fs, :])
    o_ref[0] = x1 + mod_ref[5:6, :] * acc


def _post_call(x, oa, ob, mod, l, gain, woa, wob, w1, w2, tm):
    b, t, d = x.shape
    full = lambda a: pl.BlockSpec(a.shape, lambda bi, i: (0,) * a.ndim, pipeline_mode=pl.Buffered(1))
    tok = lambda w: pl.BlockSpec((1, tm, w), lambda bi, i: (bi, i, 0))
    return pl.pallas_call(
        functools.partial(_post_kernel, ff_chunk=1024),
        grid=(b, t // tm),
        in_specs=[
            tok(d), tok(SB_WIDTH), tok(DN_WIDTH),
            pl.BlockSpec((None, None, N_MOD, d), lambda bi, i: (l, bi, 0, 0)),
            full(gain), full(woa), full(wob), full(w1), full(w2),
        ],
        out_specs=tok(d),
        out_shape=jax.ShapeDtypeStruct((b, t, d), F32),
        compiler_params=pltpu.CompilerParams(
            dimension_semantics=("parallel", "parallel"), vmem_limit_bytes=VMEM_LIMIT),
        name="outproj_mlp",
    )(x, oa, ob, mod, gain, woa, wob, w1, w2)


def kernel(x, c, w_ada, b_ada, norm_mix, norm_mlp, w_in, sb_q_norm, sb_k_norm, conv_w, a_log, dt_bias,
           dn_out_norm, w_out, w_ff1, w_ff2):
    b, t, d = x.shape
    depth = w_ada.shape[0]
    tm = 512
    tq = 256
    tt = 512
    cum_rows = 256

    mod = _ada_call(c, w_ada, b_ada).reshape(depth, b, N_MOD, d)

    hi = jnp.arange(SB_WIDTH) // SB_HEAD_DIM
    blk = jnp.where(hi[:, None] == hi[None, :], 1.0 / SB_HEAD_DIM, 0.0).astype(BF16)
    ti = jnp.arange(cum_rows)
    cum = ((ti[:, None] // DN_CHUNK == ti[None, :] // DN_CHUNK) & (ti[None, :] <= ti[:, None])).astype(BF16)
    qi = jnp.arange(tq)
    u = (qi[:, None] >= qi[None, :]).astype(BF16)
    pad = jnp.zeros((AB_PAD - DN_HEADS,), F32)

    c1, c2, c3, c4 = SB_WIDTH * 3, SB_WIDTH * 3 + DN_WIDTH * 3, SB_WIDTH * 3 + DN_WIDTH * 4, w_in.shape[2]
    for l in range(depth):
        wl = w_in[l]
        wsb = wl[:, :c1].astype(BF16)
        wdn = wl[:, c1:c2].astype(BF16)
        wz = wl[:, c2:c3].astype(BF16)
        wab = jnp.pad(wl[:, c3:c4], ((0, 0), (0, AB_PAD - (c4 - c3)))).astype(BF16)
        qg = jnp.tile(sb_q_norm[l], SB_HEADS)[None, :]
        kg = jnp.tile(sb_k_norm[l], SB_HEADS)[None, :]
        alog = jnp.concatenate([a_log[l], pad])[None, :]
        dtb = jnp.concatenate([dt_bias[l], pad])[None, :]
        q, k, v, dn, z, gbc, gbr = _premix_call(x, mod, l, norm_mix[l][None, :], wsb, wdn, wz, wab, qg, kg,
                                                blk, alog, dtb, cum, conv_w[l], tm)
        oa = _sb_attn_call(q, k, v, u, tq)
        gbr5 = gbr.reshape(b, 2 * DN_HEADS, t // DN_CHUNK, 1, DN_CHUNK)
        ob = _gdn_call(dn, z, gbc, gbr5, dn_out_norm[l][None, :], tt)
        x = _post_call(x, oa, ob, mod, l, norm_mlp[l][None, :],
                       w_out[l, :SB_WIDTH].astype(BF16), w_out[l, SB_WIDTH:].astype(BF16),
                       w_ff1[l].astype(BF16), w_ff2[l].astype(BF16), tm)
    return x
```

```python
import functools

import jax
import jax.numpy as jnp
from jax import lax
from jax.experimental import pallas as pl
from jax.experimental.pallas import tpu as pltpu

F32 = jnp.float32
BF16 = jnp.bfloat16
EPS = 1e-6

SB_HEADS = 8
SB_HEAD_DIM = 64
SB_WIDTH = SB_HEADS * SB_HEAD_DIM
DN_HEADS = 4
DN_HEAD_DIM = 128
DN_WIDTH = DN_HEADS * DN_HEAD_DIM
DN_CHUNK = 64
CONV_WIDTH = 4
N_MOD = 6
LANES = 128
AB_PAD = LANES
VMEM_LIMIT = 56 * 1024 * 1024
LOG2E = 1.4426950408889634
SB_SKIP_LOG2 = 100.0 * LOG2E

NN = (((1,), (0,)), ((), ()))
NT = (((1,), (1,)), ((), ()))
TN = (((0,), (0,)), ((), ()))


def _dot(a, b, dims=NN):
    return lax.dot_general(a, b, dims, preferred_element_type=F32)


def _split2(a):
    hi = a.astype(BF16)
    lo = (a - hi.astype(F32)).astype(BF16)
    return hi, lo


def _split3(a):
    hi = a.astype(BF16)
    r = a - hi.astype(F32)
    mid = r.astype(BF16)
    lo = (r - mid.astype(F32)).astype(BF16)
    return hi, mid, lo


def _dot_x3(a, b, dims=NN):
    ah, al = _split2(a)
    bh, bl = _split2(b)
    return _dot(ah, bh, dims) + (_dot(ah, bl, dims) + _dot(al, bh, dims))


def _dot_x1(a, b, dims=NN):
    return _dot(a.astype(BF16), b.astype(BF16), dims)


def _bmm(a, b, dims, dot):
    return jnp.stack([dot(a[g], b[g], dims) for g in range(a.shape[0])])


GDN_DOT_SCORES = _dot_x1
GDN_DOT_INVERSE = _dot_x1
GDN_DOT_SOLVE = _dot_x1
GDN_DOT_CHUNK = _dot_x1
GDN_DOT_STATE = _dot_x1
GDN_DOT_OUT = _dot_x1


def _sigmoid(x):
    return 1.0 / (1.0 + jnp.exp(-x))


def _softplus(x):
    return jnp.maximum(x, 0.0) + jnp.log1p(jnp.exp(-jnp.abs(x)))


def _ada_kernel(c_ref, w_ref, b_ref, o_ref):
    c = c_ref[...]
    cond = c * _sigmoid(c)
    o_ref[0] = _dot_x3(cond, w_ref[0]) + b_ref[0]


def _ada_call(c, w_ada, b_ada):
    depth, d, n = w_ada.shape
    b = c.shape[0]
    tn = 1536
    return pl.pallas_call(
        _ada_kernel,
        grid=(depth, n // tn),
        in_specs=[
            pl.BlockSpec((b, d), lambda l, j: (0, 0)),
            pl.BlockSpec((1, d, tn), lambda l, j: (l, 0, j)),
            pl.BlockSpec((1, 1, tn), lambda l, j: (l, 0, j)),
        ],
        out_specs=pl.BlockSpec((1, b, tn), lambda l, j: (l, 0, j)),
        out_shape=jax.ShapeDtypeStruct((depth, b, n), F32),
        compiler_params=pltpu.CompilerParams(
            dimension_semantics=("arbitrary", "arbitrary"), vmem_limit_bytes=VMEM_LIMIT),
        name="adaln_mod",
    )(c, w_ada, b_ada.reshape(depth, 1, n))


def _premix_kernel(x_ref, mod_ref, gain_ref, wsb_ref, wdn_ref, wz_ref, wab_ref, qg_ref, kg_ref,
                   blk_ref, alog_ref, dtb_ref, cum_ref, cw_ref,
                   q_ref, k_ref, v_ref, dn_ref, z_ref, gbc_ref, gbr_ref, xbuf):
    tm = x_ref.shape[1]
    x = x_ref[0]
    ms = jnp.mean(x * x, axis=-1, keepdims=True)
    y = x * lax.rsqrt(ms + EPS) * gain_ref[...]
    h = y * (1.0 + mod_ref[1:2, :]) + mod_ref[0:1, :]
    hb = h.astype(BF16)

    @pl.when(pl.program_id(1) == 0)
    def _():
        xbuf[0:8, :] = jnp.zeros((8, 3 * DN_WIDTH), F32)

    hd = DN_HEAD_DIM
    gw = 2 * hd
    n_groups = 3 * DN_WIDTH // gw

    def dn_group(j):
        xbuf[8:8 + tm, j * gw:(j + 1) * gw] = _dot(hb, wdn_ref[:, j * gw:(j + 1) * gw])

    dn_group(0)
    sb_groups = []
    for j in range(n_groups):
        if j + 1 < n_groups:
            dn_group(j + 1)
        sb_groups.append(_dot(hb, wsb_ref[:, j * gw:(j + 1) * gw]))
        for cb in (2 * j, 2 * j + 1):
            ls = slice(cb * hd, (cb + 1) * hd)
            acc = cw_ref[CONV_WIDTH - 1:CONV_WIDTH, ls] * xbuf[8:8 + tm, ls]
            for kk in range(CONV_WIDTH - 1):
                off = 8 - (CONV_WIDTH - 1) + kk
                acc = acc + cw_ref[kk:kk + 1, ls] * xbuf[off:off + tm, ls]
            act = acc * _sigmoid(acc)
            if cb < 2 * DN_HEADS:
                act = act * lax.rsqrt(jnp.sum(act * act, axis=-1, keepdims=True) + EPS)
            if cb < DN_HEADS:
                act = act * (hd ** -0.5)
            dn_ref[0, :, ls] = act
    xbuf[0:8, :] = xbuf[tm:tm + 8, :]
    sb = jnp.concatenate(sb_groups, axis=-1)

    z_ref[0] = _dot(hb, wz_ref[...])
    ab = _dot(hb, wab_ref[...])

    blk = blk_ref[...]

    def head_rmsnorm(a, g):
        m = _dot((a * a).astype(BF16), blk)
        return a * lax.rsqrt(m + EPS) * g

    q = head_rmsnorm(sb[:, :SB_WIDTH], qg_ref[...]) * (SB_HEAD_DIM ** -0.5 * LOG2E)
    k = head_rmsnorm(sb[:, SB_WIDTH:2 * SB_WIDTH], kg_ref[...])
    q_ref[0] = q.astype(BF16)
    k_ref[0] = k.astype(BF16)
    v_ref[0] = sb[:, 2 * SB_WIDTH:].astype(BF16)

    lane =lax.broadcasted_iota(jnp.int32, ab.shape, 1)
    g = -jnp.exp(alog_ref[...]) * _softplus(ab + dtb_ref[...])
    cum = cum_ref[...]
    rows = cum.shape[0]
    gcs = []
    for r0 in range(0, g.shape[0], rows):
        g1, g2, g3 = _split3(g[r0:r0 + rows])
        gcs.append(_dot(cum, g1) + (_dot(cum, g2) + _dot(cum, g3)))
    gc = jnp.concatenate(gcs, axis=0)
    gb = jnp.where(lane < DN_HEADS, gc, _sigmoid(ab))
    gbc_ref[0] = gb
    gbr_ref[0] = gb.T[0:2 * DN_HEADS, :]


def _premix_call(x, mod, l, gain, wsb, wdn, wz, wab, qg, kg, blk, alog, dtb, cum, conv_w, tm):
    b, t, d = x.shape
    full = lambda a: pl.BlockSpec(a.shape, lambda bi, i: (0,) * a.ndim, pipeline_mode=pl.Buffered(1))
    tok = lambda w: pl.BlockSpec((1, tm, w), lambda bi, i: (bi, i, 0))
    return pl.pallas_call(
        _premix_kernel,
        grid=(b, t // tm),
        in_specs=[
            tok(d),
            pl.BlockSpec((None, None, N_MOD, d), lambda bi, i: (l, bi, 0, 0)),
            full(gain), full(wsb), full(wdn), full(wz), full(wab), full(qg), full(kg), full(blk),
            full(alog), full(dtb), full(cum), full(conv_w),
        ],
        out_specs=[tok(SB_WIDTH), tok(SB_WIDTH), tok(SB_WIDTH), tok(3 * DN_WIDTH), tok(DN_WIDTH),
                   tok(AB_PAD),
                   pl.BlockSpec((1, 2 * DN_HEADS, tm), lambda bi, i: (bi, 0, i))],
        out_shape=[
            jax.ShapeDtypeStruct((b, t, SB_WIDTH), BF16),
            jax.ShapeDtypeStruct((b, t, SB_WIDTH), BF16),
            jax.ShapeDtypeStruct((b, t, SB_WIDTH), BF16),
            jax.ShapeDtypeStruct((b, t, 3 * DN_WIDTH), F32),
            jax.ShapeDtypeStruct((b, t, DN_WIDTH), F32),
            jax.ShapeDtypeStruct((b, t, AB_PAD), F32),
            jax.ShapeDtypeStruct((b, 2 * DN_HEADS, t), F32),
        ],
        scratch_shapes=[pltpu.VMEM((tm + 8, 3 * DN_WIDTH), F32)],
        compiler_params=pltpu.CompilerParams(
            dimension_semantics=("parallel", "arbitrary"), vmem_limit_bytes=VMEM_LIMIT),
        name="premix_proj",
    )(x, mod, gain, wsb, wdn, wz, wab, qg, kg, blk, alog, dtb, cum, conv_w)


SB_PAIRS = 4


def _sb_softplus2(z):
    return jnp.maximum(z, 0.0) + jnp.log(1.0 + jnp.exp2(-jnp.abs(z))) * LOG2E


def _sb_attn_kernel(q_ref, k_ref, v_ref, u_ref, o_ref, *, tq):
    i = pl.program_id(2)
    lane = lax.broadcasted_iota(jnp.int32, (1, LANES), 1)
    first = lane < SB_HEAD_DIM
    q2s, lanes = [], []
    for p in range(SB_PAIRS):
        q = q_ref[0, :, p * LANES:(p + 1) * LANES]
        zero = jnp.zeros_like(q)
        q2s.append(jnp.concatenate([jnp.where(first, q, zero), jnp.where(first, zero, q)], axis=0))
        lanes.append(slice(p * LANES, (p + 1) * LANES))
    n_heads = 2 * SB_PAIRS
    u = u_ref[...]
    row = lax.broadcasted_iota(jnp.int32, (tq, tq), 0)
    col = lax.broadcasted_iota(jnp.int32, (tq, tq), 1)
    causal = col < row

    def kv_block(j, ls):
        start = pl.multiple_of(j * tq, tq)
        return k_ref[0, pl.ds(start, tq), ls], v_ref[0, pl.ds(start, tq), ls]

    def scores(j):
        zs, vbs = [], []
        for p in range(SB_PAIRS):
            kb, vb = kv_block(j, lanes[p])
            z2 = _dot(q2s[p], kb, NT)
            zs += [z2[:tq], z2[tq:]]
            vbs.append(vb)
        return zs, vbs

    def suffix_sums(spbs):
        incl = _dot(jnp.concatenate(spbs, axis=0), u)
        return [incl[n * tq:(n + 1) * tq] for n in range(len(spbs))]

    def weighted_values(atts, vbs):
        pvs = []
        for p in range(SB_PAIRS):
            pv2 = _dot(jnp.concatenate(atts[2 * p:2 * p + 2], axis=0), vbs[p])
            pvs += [pv2[:tq], pv2[tq:]]
        return pvs

    has_prev = jnp.where(i > 0, 1.0, 0.0)
    z_d, v_d = scores(i)
    z_p, v_p = scores(jnp.maximum(i - 1, 0))
    spb_d = [jnp.where(causal, _sb_softplus2(z), 0.0).astype(BF16) for z in z_d]
    spb_p = [_sb_softplus2(z).astype(BF16) for z in z_p]
    incls = suffix_sums(spb_d + spb_p)
    incl_d, incl_p = incls[:n_heads], incls[n_heads:]
    att_d = [jnp.where(causal, jnp.exp2(z - incl), 0.0).astype(BF16) for z, incl in zip(z_d, incl_d)]
    att_p = [jnp.exp2(z - incl - tot[:, 0:1]).astype(BF16) for z, incl, tot in zip(z_p, incl_p, incl_d)]
    pv_d = weighted_values(att_d, v_d)
    pv_p = weighted_values(att_p, v_p)
    accs = tuple(d + has_prev * p for d, p in zip(pv_d, pv_p))
    cs = tuple(d[:, 0:1] + has_prev * p[:, 0:1] for d, p in zip(incl_d, incl_p))

    def cond(st):
        j, _, c = st
        cmin = c[0]
        for cc in c[1:]:
            cmin = jnp.minimum(cmin, cc)
        return jnp.logical_and(j >= 0, jnp.min(cmin) < SB_SKIP_LOG2)

    def body(st):
        j, a, c = st
        zs, vbs = scores(j)
        incl = suffix_sums([_sb_softplus2(z).astype(BF16) for z in zs])
        atts = [jnp.exp2(z - inc - cc).astype(BF16) for z, inc, cc in zip(zs, incl, c)]
        pvs = weighted_values(atts, vbs)
        return (j - 1, tuple(aa + pv for aa, pv in zip(a, pvs)),
                tuple(cc + inc[:, 0:1] for cc, inc in zip(c, incl)))

    _, accs, _ = lax.while_loop(cond, body, (i - 2, accs, cs))
    for p in range(SB_PAIRS):
        o_ref[0, :, p * LANES:(p + 1) * LANES] = jnp.where(first, accs[2 * p], accs[2 * p + 1]).astype(BF16)


def _sb_attn_call(q, k, v, u, tq):
    b, t, _ = q.shape
    w = SB_PAIRS * LANES
    return pl.pallas_call(
        functools.partial(_sb_attn_kernel, tq=tq),
        grid=(b, SB_WIDTH // w, t // tq),
        in_specs=[
            pl.BlockSpec((1, tq, w), lambda bi, p, i: (bi, i, p)),
            pl.BlockSpec((1, t, w), lambda bi, p, i: (bi, 0, p)),
            pl.BlockSpec((1, t, w), lambda bi, p, i: (bi, 0, p)),
            pl.BlockSpec((tq, tq), lambda bi, p, i: (0, 0)),
        ],
        out_specs=pl.BlockSpec((1, tq, w), lambda bi, p, i: (bi, i, p)),
        out_shape=jax.ShapeDtypeStruct((b, t, SB_WIDTH), BF16),
        compiler_params=pltpu.CompilerParams(
            dimension_semantics=("parallel", "parallel", "arbitrary"), vmem_limit_bytes=VMEM_LIMIT),
        name="sb_attention",
    )(q, k, v, u)


def _unit_lower_inverse(a, ri, ci):
    mm = lambda p_, q_: _bmm(p_, q_, NN, GDN_DOT_INVERSE)
    same = ri == ci
    x = None
    for level in range(1, 7):
        same_next = (ri >> level) == (ci >> level)
        off = jnp.where(jnp.logical_and(same_next, jnp.logical_not(same)), a, 0.0)
        if x is None:
            x = jnp.where(ri == ci, 1.0, 0.0) - off
        else:
            x = x - mm(x, mm(off, x))
        same = same_next
    return x


def _gdn_kernel(qkv_ref, z_ref, gbc_ref, gbr_ref, on_ref, o_ref, state, *, tt):
    c_sz = DN_CHUNK
    hd = DN_HEAD_DIM
    nc = tt // c_sz
    qkv_s = qkv_ref.at[0]

    @pl.when(pl.program_id(1) == 0)
    def _():
        state[...] = jnp.zeros_like(state)

    ri = lax.broadcasted_iota(jnp.int32, (c_sz, c_sz), 0)
    ci = lax.broadcasted_iota(jnp.int32, (c_sz, c_sz), 1)
    tril_incl = ci <= ri
    tril_strict = ci < ri

    gcol = gbc_ref[0].reshape(nc, c_sz, AB_PAD)
    heads = range(DN_HEADS)

    def cat(f):
        return jnp.concatenate([f(h) for h in heads], axis=0)

    q = cat(lambda h: qkv_s[:, h * hd:(h + 1) * hd].reshape(nc, c_sz, hd))
    k = cat(lambda h: qkv_s[:, DN_WIDTH + h * hd:DN_WIDTH + (h + 1) * hd].reshape(nc, c_sz, hd))
    v = cat(lambda h: qkv_s[:, 2 * DN_WIDTH + h * hd:2 * DN_WIDTH + (h + 1) * hd].reshape(nc, c_sz, hd))
    gc_col = cat(lambda h: gcol[:, :, h:h + 1])
    beta_col = cat(lambda h: gcol[:, :, DN_HEADS + h:DN_HEADS + h + 1])
    gc_row = gbr_ref[0, 0:DN_HEADS].reshape(DN_HEADS * nc, 1, c_sz)
    g_last = gc_row[:, :, c_sz - 1:c_sz]
    decay = jnp.where(tril_incl, jnp.exp(jnp.where(tril_incl, gc_col - gc_row, 0.0)), 0.0)
    k_beta = k * beta_col
    a = jnp.where(tril_strict, _bmm(k_beta, k, NT, GDN_DOT_SCORES) * decay, 0.0)
    t_mat = _unit_lower_inverse(a, ri, ci)
    rhs = jnp.concatenate([v * beta_col, k_beta * jnp.exp(gc_col)], axis=-1)
    uw = _bmm(t_mat, rhs, NN, GDN_DOT_SOLVE)
    intra = jnp.where(tril_incl, _bmm(q, k, NT, GDN_DOT_SCORES) * decay, 0.0)
    kd = k * jnp.exp(g_last - gc_col)
    iuw = _bmm(intra, uw, NN, GDN_DOT_CHUNK)
    kuw = _bmm(kd, uw, TN, GDN_DOT_CHUNK)
    q_eff = q * jnp.exp(gc_col) - iuw[:, :, hd:]
    o_base = iuw[:, :, :hd]
    e_last = jnp.exp(g_last)
    kw = kuw[:, :, hd:]
    n_mat = kuw[:, :, :hd]

    s_cur = [state[h] for h in heads]
    s_before = [[None] * nc for _ in heads]
    for c in range(nc):
        for h in heads:
            g = h * nc + c
            s_before[h][c] = s_cur[h]
            s_cur[h] = (e_last[g] * s_cur[h] - GDN_DOT_STATE(kw[g], s_cur[h])) + n_mat[g]
    for h in heads:
        state[h] = s_cur[h]
    s_all = jnp.stack([s_before[h][c] for h in heads for c in range(nc)])
    o_all = _bmm(q_eff, s_all, NN, GDN_DOT_OUT) + o_base
    outs = []
    for h in heads:
        o = o_all[h * nc:(h + 1) * nc].reshape(tt, hd)
        ms = jnp.mean(o * o, axis=-1, keepdims=True)
        zz = z_ref[0, :, h * hd:(h + 1) * hd]
        outs.append((o * lax.rsqrt(ms + EPS) * on_ref[...] * (zz * _sigmoid(zz))).astype(BF16))
    o_ref[0] = jnp.concatenate(outs, axis=-1)


def _gdn_call(dn, z, gbc, gbr5, out_norm, tt):
    b, t, _ = dn.shape
    nc = tt // DN_CHUNK
    return pl.pallas_call(
        functools.partial(_gdn_kernel, tt=tt),
        grid=(b, t // tt),
        in_specs=[
            pl.BlockSpec((1, tt, 3 * DN_WIDTH), lambda bi, i: (bi, i, 0)),
            pl.BlockSpec((1, tt, DN_WIDTH), lambda bi, i: (bi, i, 0)),
            pl.BlockSpec((1, tt, AB_PAD), lambda bi, i: (bi, i, 0)),
            pl.BlockSpec((1, 2 * DN_HEADS, nc, 1, DN_CHUNK), lambda bi, i: (bi, 0, i, 0, 0)),
            pl.BlockSpec(out_norm.shape, lambda bi, i: (0, 0)),
        ],
        out_specs=pl.BlockSpec((1, tt, DN_WIDTH), lambda bi, i: (bi, i, 0)),
        out_shape=jax.ShapeDtypeStruct((b, t, DN_WIDTH), BF16),
        scratch_shapes=[
            pltpu.VMEM((DN_HEADS, DN_HEAD_DIM, DN_HEAD_DIM), F32),
        ],
        compiler_params=pltpu.CompilerParams(
            dimension_semantics=("parallel", "arbitrary"), vmem_limit_bytes=VMEM_LIMIT),
        name="gated_deltanet",
    )(dn, z, gbc, gbr5, out_norm)


def _post_kernel(x_ref, oa_ref, ob_ref, mod_ref, gain_ref, woa_ref, wob_ref, w1_ref, w2_ref, o_ref, *, ff_chunk):
    mix = _dot(oa_ref[0], woa_ref[...]) + _dot(ob_ref[0], wob_ref[...])
    x1 = x_ref[0] + mod_ref[2:3, :] * mix
    ms = jnp.mean(x1 * x1, axis=-1, keepdims=True)
    y = x1 * lax.rsqrt(ms + EPS) * gain_ref[...]
    hb = (y * (1.0 + mod_ref[4:5, :]) + mod_ref[3:4, :]).astype(BF16)
    d_ff = w1_ref.shape[1]
    acc = jnp.zeros(x1.shape, F32)
    for f in range(d_ff // ff_chunk):
        fs = slice(f * ff_chunk, (f + 1) * ff_chunk)
        hid = jnp.maximum(_dot(hb, w1_ref[:, fs]), 0.0)
        acc = acc + _dot((hid * hid).astype(BF16), w2_ref[fs, :])
    o_ref[0] = x1 + mod_ref[5:6, :] * acc


def _post_call(x, oa, ob, mod, l, gain, woa, wob, w1, w2, tm):
    b, t, d = x.shape
    full = lambda a: pl.BlockSpec(a.shape, lambda bi, i: (0,) * a.ndim, pipeline_mode=pl.Buffered(1))
    tok = lambda w: pl.BlockSpec((1, tm, w), lambda bi, i: (bi, i, 0))
    return pl.pallas_call(
        functools.partial(_post_kernel, ff_chunk=1024),
        grid=(b, t // tm),
        in_specs=[
            tok(d), tok(SB_WIDTH), tok(DN_WIDTH),
            pl.BlockSpec((None, None, N_MOD, d), lambda bi, i: (l, bi, 0, 0)),
            full(gain), full(woa), full(wob), full(w1), full(w2),
        ],
        out_specs=tok(d),
        out_shape=jax.ShapeDtypeStruct((b, t, d), F32),
        compiler_params=pltpu.CompilerParams(
            dimension_semantics=("parallel", "parallel"), vmem_limit_bytes=VMEM_LIMIT),
        name="outproj_mlp",
    )(x, oa, ob, mod, gain, woa, wob, w1, w2)


def kernel(x, c, w_ada, b_ada, norm_mix, norm_mlp, w_in, sb_q_norm, sb_k_norm, conv_w, a_log, dt_bias,
           dn_out_norm, w_out, w_ff1, w_ff2):
    b, t, d = x.shape
    depth = w_ada.shape[0]
    tm = 512
    tq = 256
    tt = 512
    cum_rows = 256

    mod = _ada_call(c, w_ada, b_ada).reshape(depth, b, N_MOD, d)

    hi = jnp.arange(SB_WIDTH) // SB_HEAD_DIM
    blk = jnp.where(hi[:, None] == hi[None, :], 1.0 / SB_HEAD_DIM, 0.0).astype(BF16)
    ti = jnp.arange(cum_rows)
    cum = ((ti[:, None] // DN_CHUNK == ti[None, :] // DN_CHUNK) & (ti[None, :] <= ti[:, None])).astype(BF16)
    qi = jnp.arange(tq)
    u = (qi[:, None] >= qi[None, :]).astype(BF16)
    pad = jnp.zeros((AB_PAD - DN_HEADS,), F32)

    c1, c2, c3, c4 = SB_WIDTH * 3, SB_WIDTH * 3 + DN_WIDTH * 3, SB_WIDTH * 3 + DN_WIDTH * 4, w_in.shape[2]
    for l in range(depth):
        wl = w_in[l]
        wsb = wl[:, :c1].astype(BF16)
        wdn = wl[:, c1:c2].astype(BF16)
        wz = wl[:, c2:c3].astype(BF16)
        wab = jnp.pad(wl[:, c3:c4], ((0, 0), (0, AB_PAD - (c4 - c3)))).astype(BF16)
        qg = jnp.tile(sb_q_norm[l], SB_HEADS)[None, :]
        kg = jnp.tile(sb_k_norm[l], SB_HEADS)[None, :]
        alog = jnp.concatenate([a_log[l], pad])[None, :]
        dtb = jnp.concatenate([dt_bias[l], pad])[None, :]
        q, k, v, dn, z, gbc, gbr = _premix_call(x, mod, l, norm_mix[l][None, :], wsb, wdn, wz, wab, qg, kg,
                                                blk, alog, dtb, cum, conv_w[l], tm)
        oa = _sb_attn_call(q, k, v, u, tq)
        gbr5 = gbr.reshape(b, 2 * DN_HEADS, t // DN_CHUNK, 1, DN_CHUNK)
        ob = _gdn_call(dn, z, gbc, gbr5, dn_out_norm[l][None, :], tt)
        x = _post_call(x, oa, ob, mod, l, norm_mlp[l][None, :],
                       w_out[l, :SB_WIDTH].astype(BF16), w_out[l, SB_WIDTH:].astype(BF16),
                       w_ff1[l].astype(BF16), w_ff2[l].astype(BF16), tm)
    return x
```

```python
import functools

import jax
import jax.numpy as jnp
from jax import lax
from jax.experimental import pallas as pl
from jax.experimental.pallas import tpu as pltpu

F32 = jnp.float32
BF16 = jnp.bfloat16
EPS = 1e-6

SB_HEADS = 8
SB_HEAD_DIM = 64
SB_WIDTH = SB_HEADS * SB_HEAD_DIM
DN_HEADS = 4
DN_HEAD_DIM = 128
DN_WIDTH = DN_HEADS * DN_HEAD_DIM
DN_CHUNK = 64
CONV_WIDTH = 4
N_MOD = 6
LANES = 128
AB_PAD = LANES
VMEM_LIMIT = 56 * 1024 * 1024
LOG2E = 1.4426950408889634
SB_SKIP_LOG2 = 100.0 * LOG2E

NN = (((1,), (0,)), ((), ()))
NT = (((1,), (1,)), ((), ()))
TN = (((0,), (0,)), ((), ()))


def _dot(a, b, dims=NN):
    return lax.dot_general(a, b, dims, preferred_element_type=F32)


def _split2(a):
    hi = a.astype(BF16)
    lo = (a - hi.astype(F32)).astype(BF16)
    return hi, lo


def _split3(a):
    hi = a.astype(BF16)
    r = a - hi.astype(F32)
    mid = r.astype(BF16)
    lo = (r - mid.astype(F32)).astype(BF16)
    return hi, mid, lo


def _dot_x3(a, b, dims=NN):
    ah, al = _split2(a)
    bh, bl = _split2(b)
    return _dot(ah, bh, dims) + (_dot(ah, bl, dims) + _dot(al, bh, dims))


def _dot_x1(a, b, dims=NN):
    return _dot(a.astype(BF16), b.astype(BF16), dims)


def _bmm(a, b, dims, dot):
    return jnp.stack([dot(a[g], b[g], dims) for g in range(a.shape[0])])


GDN_DOT_SCORES = _dot_x1
GDN_DOT_INVERSE = _dot_x1
GDN_DOT_SOLVE = _dot_x1
GDN_DOT_CHUNK = _dot_x1
GDN_DOT_STATE = _dot_x1
GDN_DOT_OUT = _dot_x1


def _sigmoid(x):
    return 1.0 / (1.0 + jnp.exp(-x))


def _silu(x):
    h = 0.5 * x
    return h + h * jnp.tanh(h)


def _softplus(x):
    return jnp.maximum(x, 0.0) + jnp.log1p(jnp.exp(-jnp.abs(x)))


def _ada_kernel(c_ref, w_ref, b_ref, o_ref):
    c = c_ref[...]
    cond = c * _sigmoid(c)
    o_ref[0] = _dot_x3(cond, w_ref[0]) + b_ref[0]


def _ada_call(c, w_ada, b_ada):
    depth, d, n = w_ada.shape
    b = c.shape[0]
    tn = 1536
    return pl.pallas_call(
        _ada_kernel,
        grid=(depth, n // tn),
        in_specs=[
            pl.BlockSpec((b, d), lambda l, j: (0, 0)),
            pl.BlockSpec((1, d, tn), lambda l, j: (l, 0, j)),
            pl.BlockSpec((1, 1, tn), lambda l, j: (l, 0, j)),
        ],
        out_specs=pl.BlockSpec((1, b, tn), lambda l, j: (l, 0, j)),
        out_shape=jax.ShapeDtypeStruct((depth, b, n), F32),
        compiler_params=pltpu.CompilerParams(
            dimension_semantics=("arbitrary", "arbitrary"), vmem_limit_bytes=VMEM_LIMIT),
        name="adaln_mod",
    )(c, w_ada, b_ada.reshape(depth, 1, n))


def _premix_kernel(x_ref, mod_ref, gain_ref, wsb_ref, wdn_ref, wz_ref, wab_ref, qg_ref, kg_ref,
                   blk_ref, alog_ref, dtb_ref, cum_ref, cw_ref,
                   q_ref, k_ref, v_ref, dn_ref, z_ref, gbc_ref, gbr_ref, xbuf):
    tm = x_ref.shape[1]
    x = x_ref[0]
    ms = jnp.mean(x * x, axis=-1, keepdims=True)
    y = x * lax.rsqrt(ms + EPS) * gain_ref[...]
    h = y * (1.0 + mod_ref[1:2, :]) + mod_ref[0:1, :]
    hb = h.astype(BF16)

    @pl.when(pl.program_id(1) == 0)
    def _():
        xbuf[0:8, :] = jnp.zeros((8, 3 * DN_WIDTH), F32)

    hd = DN_HEAD_DIM
    gw = 2 * hd
    n_groups = 3 * DN_WIDTH // gw

    def dn_group(j):
        xbuf[8:8 + tm, j * gw:(j + 1) * gw] = _dot(hb, wdn_ref[:, j * gw:(j + 1) * gw])

    dn_group(0)
    sb_groups = []
    ab = None
    for j in range(n_groups):
        if j + 1 < n_groups:
            dn_group(j + 1)
        sb_groups.append(_dot(hb, wsb_ref[:, j * gw:(j + 1) * gw]))
        if j < DN_WIDTH // gw:
            z_ref[0, :, j * gw:(j + 1) * gw] = _dot(hb, wz_ref[:, j * gw:(j + 1) * gw])
        elif ab is None:
            ab = _dot(hb, wab_ref[...])
        for cb in (2 * j, 2 * j + 1):
            ls = slice(cb * hd, (cb + 1) * hd)
            acc = cw_ref[CONV_WIDTH - 1:CONV_WIDTH, ls] * xbuf[8:8 + tm, ls]
            for kk in range(CONV_WIDTH - 1):
                off = 8 - (CONV_WIDTH - 1) + kk
                acc = acc + cw_ref[kk:kk + 1, ls] * xbuf[off:off + tm, ls]
            act = _silu(acc)
            if cb < 2 * DN_HEADS:
                act = act * lax.rsqrt(jnp.sum(act * act, axis=-1, keepdims=True) + EPS)
            if cb < DN_HEADS:
                act = act * (hd ** -0.5)
            dn_ref[0, :, ls] = act
    xbuf[0:8, :] = xbuf[tm:tm + 8, :]
    sb = jnp.concatenate(sb_groups, axis=-1)

    blk = blk_ref[...]

    def head_rmsnorm(a, g):
        m = _dot((a * a).astype(BF16), blk)
        return a * lax.rsqrt(m + EPS) * g

    q = head_rmsnorm(sb[:, :SB_WIDTH], qg_ref[...]) * (SB_HEAD_DIM ** -0.5 * LOG2E)
    k = head_rmsnorm(sb[:, SB_WIDTH:2 * SB_WIDTH], kg_ref[...])
    q_ref[0] = q.astype(BF16)
    k_ref[0] = k.astype(BF16)
    v_ref[0] = sb[:, 2 * SB_WIDTH:].astype(BF16)

    lane =lax.broadcasted_iota(jnp.int32, ab.shape, 1)
    g = -jnp.exp(alog_ref[...]) * _softplus(ab + dtb_ref[...])
    cum = cum_ref[...]
    rows = cum.shape[0]
    gcs = []
    for r0 in range(0, g.shape[0], rows):
        g1, g2, g3 = _split3(g[r0:r0 + rows])
        gcs.append(_dot(cum, g1) + (_dot(cum, g2) + _dot(cum, g3)))
    gc = jnp.concatenate(gcs, axis=0)
    gb = jnp.where(lane < DN_HEADS, gc, _sigmoid(ab))
    gbc_ref[0] = gb
    gbr_ref[0] = gb.T[0:2 * DN_HEADS, :]


def _layer_spec(l, rows, cols, col_block=0):
    return pl.BlockSpec((None, rows, cols), lambda bi, i: (l, 0, col_block), pipeline_mode=pl.Buffered(1))


def _premix_call(x, mod, l, gain, w_in, wab, qg, kg, blk, alog, dtb, cum, conv_w, tm):
    b, t, d = x.shape
    full = lambda a: pl.BlockSpec(a.shape, lambda bi, i: (0,) * a.ndim, pipeline_mode=pl.Buffered(1))
    tok = lambda w: pl.BlockSpec((1, tm, w), lambda bi, i: (bi, i, 0))
    lay = lambda a: _layer_spec(l, a.shape[1], a.shape[2])
    sbw, dnw = 3 * SB_WIDTH, 3 * DN_WIDTH
    assert sbw == dnw and (sbw + dnw) % DN_WIDTH == 0
    return pl.pallas_call(
        _premix_kernel,
        grid=(b, t // tm),
        in_specs=[
            tok(d),
            pl.BlockSpec((None, None, N_MOD, d), lambda bi, i: (l, bi, 0, 0)),
            lay(gain), _layer_spec(l, d, sbw, 0), _layer_spec(l, d, dnw, 1),
            _layer_spec(l, d, DN_WIDTH, (sbw + dnw) // DN_WIDTH), lay(wab), lay(qg), lay(kg), full(blk),
            lay(alog), lay(dtb), full(cum), lay(conv_w),
        ],
        out_specs=[tok(SB_WIDTH), tok(SB_WIDTH), tok(SB_WIDTH), tok(3 * DN_WIDTH), tok(DN_WIDTH),
                   tok(AB_PAD),
                   pl.BlockSpec((1, 2 * DN_HEADS, tm), lambda bi, i: (bi, 0, i))],
        out_shape=[
            jax.ShapeDtypeStruct((b, t, SB_WIDTH), BF16),
            jax.ShapeDtypeStruct((b, t, SB_WIDTH), BF16),
            jax.ShapeDtypeStruct((b, t, SB_WIDTH), BF16),
            jax.ShapeDtypeStruct((b, t, 3 * DN_WIDTH), F32),
            jax.ShapeDtypeStruct((b, t, DN_WIDTH), F32),
            jax.ShapeDtypeStruct((b, t, AB_PAD), F32),
            jax.ShapeDtypeStruct((b, 2 * DN_HEADS, t), F32),
        ],
        scratch_shapes=[pltpu.VMEM((tm + 8, 3 * DN_WIDTH), F32)],
        compiler_params=pltpu.CompilerParams(
            dimension_semantics=("parallel", "arbitrary"), vmem_limit_bytes=VMEM_LIMIT),
        name="premix_proj",
    )(x, mod, gain, w_in, w_in, w_in, wab, qg, kg, blk, alog, dtb, cum, conv_w)


SB_PAIRS = 4


def _sb_softplus2(z):
    return jnp.maximum(z, 0.0) + jnp.log(1.0 + jnp.exp2(-jnp.abs(z))) * LOG2E


def _sb_attn_kernel(q_ref, k_ref, v_ref, u_ref, o_ref, *, tq):
    i = pl.program_id(2)
    lane = lax.broadcasted_iota(jnp.int32, (1, LANES), 1)
    first = lane < SB_HEAD_DIM
    q2s, lanes = [], []
    for p in range(SB_PAIRS):
        q = q_ref[0, :, p * LANES:(p + 1) * LANES]
        zero = jnp.zeros_like(q)
        q2s.append(jnp.concatenate([jnp.where(first, q, zero), jnp.where(first, zero, q)], axis=0))
        lanes.append(slice(p * LANES, (p + 1) * LANES))
    n_heads = 2 * SB_PAIRS
    u = u_ref[...]
    hq = tq // 2
    tri = (lax.broadcasted_iota(jnp.int32, (hq, hq), 1)
           < lax.broadcasted_iota(jnp.int32, (hq, hq), 0))
    zeros_q = jnp.zeros((hq, hq), BF16)

    def quadrants(a):
        return a[:hq, :hq], a[hq:, :hq], a[hq:, hq:]

    def diag_tile(tl, bl, br):
        tl = jnp.where(tri, tl, 0.0).astype(BF16)
        br = jnp.where(tri, br, 0.0).astype(BF16)
        return jnp.concatenate([jnp.concatenate([tl, zeros_q], axis=1),
                                jnp.concatenate([bl.astype(BF16), br], axis=1)], axis=0)

    def kv_block(j, ls):
        start = pl.multiple_of(j * tq, tq)
        return k_ref[0, pl.ds(start, tq), ls], v_ref[0, pl.ds(start, tq), ls]

    def scores(j):
        zs, vbs = [], []
        for p in range(SB_PAIRS):
            kb, vb = kv_block(j, lanes[p])
            z2 = _dot(q2s[p], kb, NT)
            zs += [z2[:tq], z2[tq:]]
            vbs.append(vb)
        return zs, vbs

    def suffix_sums(spbs):
        suf = _dot(jnp.concatenate(spbs, axis=0), u)
        sufs = [suf[n * tq:(n + 1) * tq] for n in range(len(spbs))]
        tots = [sf[:, 0:1] + spb[:, 0:1].astype(F32) for sf, spb in zip(sufs, spbs)]
        return sufs, tots

    def weighted_values(atts, vbs):
        pvs = []
        for p in range(SB_PAIRS):
            pv2 = _dot(jnp.concatenate(atts[2 * p:2 * p + 2], axis=0), vbs[p])
            pvs += [pv2[:tq], pv2[tq:]]
        return pvs

    has_prev = jnp.where(i > 0, 1.0, 0.0)
    z_d, v_d = scores(i)
    z_p, v_p = scores(jnp.maximum(i - 1, 0))
    zq_d = [quadrants(z) for z in z_d]
    spq_d = [[_sb_softplus2(zz) for zz in zq] for zq in zq_d]
    lbq_d = [[zz - sp for zz, sp in zip(zq, spq)] for zq, spq in zip(zq_d, spq_d)]
    sp_p = [_sb_softplus2(z) for z in z_p]
    lb_p = [z - sp for z, sp in zip(z_p, sp_p)]
    sufs, tots = suffix_sums([diag_tile(*spq) for spq in spq_d] + [sp.astype(BF16) for sp in sp_p])
    suf_d, suf_p = sufs[:n_heads], sufs[n_heads:]
    tot_d, tot_p = tots[:n_heads], tots[n_heads:]
    att_d = [diag_tile(*[jnp.exp2(lb - sf) for lb, sf in zip(lbq, quadrants(suf))])
             for lbq, suf in zip(lbq_d, suf_d)]
    att_p = [jnp.exp2(lb - suf - tot).astype(BF16) for lb, suf, tot in zip(lb_p, suf_p, tot_d)]
    pv_d = weighted_values(att_d, v_d)
    pv_p = weighted_values(att_p, v_p)
    accs = tuple(d + has_prev * p for d, p in zip(pv_d, pv_p))
    cs = tuple(d + has_prev * p for d, p in zip(tot_d, tot_p))

    def cond(st):
        j, _, c = st
        cmin = c[0]
        for cc in c[1:]:
            cmin = jnp.minimum(cmin, cc)
        return jnp.logical_and(j >= 0, jnp.min(cmin) < SB_SKIP_LOG2)

    def body(st):
        j, a, c = st
        zs, vbs = scores(j)
        sps = [_sb_softplus2(z) for z in zs]
        sufs, tots = suffix_sums([sp.astype(BF16) for sp in sps])
        atts = [jnp.exp2((z - sp) - suf - cc).astype(BF16) for z, sp, suf, cc in zip(zs, sps, sufs, c)]
        pvs = weighted_values(atts, vbs)
        return (j - 1, tuple(aa + pv for aa, pv in zip(a, pvs)),
                tuple(cc + tot for cc, tot in zip(c, tots)))

    _, accs, _ = lax.while_loop(cond, body, (i - 2, accs, cs))
    for p in range(SB_PAIRS):
        o_ref[0, :, p * LANES:(p + 1) * LANES] = jnp.where(first, accs[2 * p], accs[2 * p + 1]).astype(BF16)


def _sb_attn_call(q, k, v, u, tq):
    b, t, _ = q.shape
    w = SB_PAIRS * LANES
    return pl.pallas_call(
        functools.partial(_sb_attn_kernel, tq=tq),
        grid=(b, SB_WIDTH // w, t // tq),
        in_specs=[
            pl.BlockSpec((1, tq, w), lambda bi, p, i: (bi, i, p)),
            pl.BlockSpec((1, t, w), lambda bi, p, i: (bi, 0, p)),
            pl.BlockSpec((1, t, w), lambda bi, p, i: (bi, 0, p)),
            pl.BlockSpec((tq, tq), lambda bi, p, i: (0, 0)),
        ],
        out_specs=pl.BlockSpec((1, tq, w), lambda bi, p, i: (bi, i, p)),
        out_shape=jax.ShapeDtypeStruct((b, t, SB_WIDTH), BF16),
        compiler_params=pltpu.CompilerParams(
            dimension_semantics=("parallel", "parallel", "arbitrary"), vmem_limit_bytes=VMEM_LIMIT),
        name="sb_attention",
    )(q, k, v, u)


def _unit_lower_inverse(a, ri, ci):
    mm = lambda p_, q_: _bmm(p_, q_, NN, GDN_DOT_INVERSE)
    same = ri == ci
    x = None
    for level in range(1, 7):
        same_next = (ri >> level) == (ci >> level)
        off = jnp.where(jnp.logical_and(same_next, jnp.logical_not(same)), a, 0.0)
        if x is None:
            x = jnp.where(ri == ci, 1.0, 0.0) - off
        else:
            x = x - mm(x, mm(off, x))
        same = same_next
    return x


def _gdn_kernel(qkv_ref, z_ref, gbc_ref, gbr_ref, on_ref, o_ref, state, *, tt):
    c_sz = DN_CHUNK
    hd = DN_HEAD_DIM
    nc = tt // c_sz
    qkv_s = qkv_ref.at[0]

    @pl.when(pl.program_id(1) == 0)
    def _():
        state[...] = jnp.zeros_like(state)

    ri = lax.broadcasted_iota(jnp.int32, (c_sz, c_sz), 0)
    ci = lax.broadcasted_iota(jnp.int32, (c_sz, c_sz), 1)
    tril_incl = ci <= ri
    tril_strict = ci < ri

    gcol = gbc_ref[0].reshape(nc, c_sz, AB_PAD)
    heads = range(DN_HEADS)

    def cat(f):
        return jnp.concatenate([f(h) for h in heads], axis=0)

    q = cat(lambda h: qkv_s[:, h * hd:(h + 1) * hd].reshape(nc, c_sz, hd))
    k = cat(lambda h: qkv_s[:, DN_WIDTH + h * hd:DN_WIDTH + (h + 1) * hd].reshape(nc, c_sz, hd))
    v = cat(lambda h: qkv_s[:, 2 * DN_WIDTH + h * hd:2 * DN_WIDTH + (h + 1) * hd].reshape(nc, c_sz, hd))
    gc_col = cat(lambda h: gcol[:, :, h:h + 1])
    beta_col = cat(lambda h: gcol[:, :, DN_HEADS + h:DN_HEADS + h + 1])
    gc_row = gbr_ref[0, 0:DN_HEADS].reshape(DN_HEADS * nc, 1, c_sz)
    g_last = gc_row[:, :, c_sz - 1:c_sz]
    decay = jnp.where(tril_incl, jnp.exp(jnp.where(tril_incl, gc_col - gc_row, 0.0)), 0.0)
    k_beta = k * beta_col
    a = jnp.where(tril_strict, _bmm(k_beta, k, NT, GDN_DOT_SCORES) * decay, 0.0)
    t_mat = _unit_lower_inverse(a, ri, ci)
    rhs = jnp.concatenate([v * beta_col, k_beta * jnp.exp(gc_col)], axis=-1)
    uw = _bmm(t_mat, rhs, NN, GDN_DOT_SOLVE)
    intra = jnp.where(tril_incl, _bmm(q, k, NT, GDN_DOT_SCORES) * decay, 0.0)
    kd = k * jnp.exp(g_last - gc_col)
    iuw = _bmm(intra, uw, NN, GDN_DOT_CHUNK)
    kuw = _bmm(kd, uw, TN, GDN_DOT_CHUNK)
    q_eff = q * jnp.exp(gc_col) - iuw[:, :, hd:]
    o_base = iuw[:, :, :hd]
    e_last = jnp.exp(g_last)
    kw = kuw[:, :, hd:]
    n_mat = kuw[:, :, :hd]

    s_cur = [state[h] for h in heads]
    s_before = [[None] * nc for _ in heads]
    for c in range(nc):
        for h in heads:
            g = h * nc + c
            s_before[h][c] = s_cur[h]
            s_cur[h] = (e_last[g] * s_cur[h] - GDN_DOT_STATE(kw[g], s_cur[h])) + n_mat[g]
    for h in heads:
        state[h] = s_cur[h]
    s_all = jnp.stack([s_before[h][c] for h in heads for c in range(nc)])
    o_all = _bmm(q_eff, s_all, NN, GDN_DOT_OUT) + o_base
    outs = []
    for h in heads:
        o = o_all[h * nc:(h + 1) * nc].reshape(tt, hd)
        ms = jnp.mean(o * o, axis=-1, keepdims=True)
        zz = z_ref[0, :, h * hd:(h + 1) * hd]
        outs.append((o * lax.rsqrt(ms + EPS) * on_ref[...] * _silu(zz)).astype(BF16))
    o_ref[0] = jnp.concatenate(outs, axis=-1)


def _gdn_call(dn, z, gbc, gbr5, out_norm, tt):
    b, t, _ = dn.shape
    nc = tt // DN_CHUNK
    return pl.pallas_call(
        functools.partial(_gdn_kernel, tt=tt),
        grid=(b, t // tt),
        in_specs=[
            pl.BlockSpec((1, tt, 3 * DN_WIDTH), lambda bi, i: (bi, i, 0)),
            pl.BlockSpec((1, tt, DN_WIDTH), lambda bi, i: (bi, i, 0)),
            pl.BlockSpec((1, tt, AB_PAD), lambda bi, i: (bi, i, 0)),
            pl.BlockSpec((1, 2 * DN_HEADS, nc, 1, DN_CHUNK), lambda bi, i: (bi, 0, i, 0, 0)),
            pl.BlockSpec(out_norm.shape, lambda bi, i: (0, 0)),
        ],
        out_specs=pl.BlockSpec((1, tt, DN_WIDTH), lambda bi, i: (bi, i, 0)),
        out_shape=jax.ShapeDtypeStruct((b, t, DN_WIDTH), BF16),
        scratch_shapes=[
            pltpu.VMEM((DN_HEADS, DN_HEAD_DIM, DN_HEAD_DIM), F32),
        ],
        compiler_params=pltpu.CompilerParams(
            dimension_semantics=("parallel", "arbitrary"), vmem_limit_bytes=VMEM_LIMIT),
        name="gated_deltanet",
    )(dn, z, gbc, gbr5, out_norm)


def _post_kernel(x_ref, oa_ref, ob_ref, mod_ref, gain_ref, woa_ref, wob_ref, w1_ref, w2_ref, o_ref, *, ff_chunk):
    mix = _dot(oa_ref[0], woa_ref[...]) + _dot(ob_ref[0], wob_ref[...])
    x1 = x_ref[0] + mod_ref[2:3, :] * mix
    ms = jnp.mean(x1 * x1, axis=-1, keepdims=True)
    y = x1 * lax.rsqrt(ms + EPS) * gain_ref[...]
    hb = (y * (1.0 + mod_ref[4:5, :]) + mod_ref[3:4, :]).astype(BF16)
    d_ff = w1_ref.shape[1]
    acc = jnp.zeros(x1.shape, F32)
    for f in range(d_ff // ff_chunk):
        fs = slice(f * ff_chunk, (f + 1) * ff_chunk)
        hid = jnp.maximum(_dot(hb, w1_ref[:, fs]), 0.0)
        acc = acc + _dot((hid * hid).astype(BF16), w2_ref[fs, :])
    o_ref[0] = x1 + mod_ref[5:6, :] * acc


def _post_call(x, oa, ob, mod, l, gain, w_out, w1, w2, tm):
    b, t, d = x.shape
    tok = lambda w: pl.BlockSpec((1, tm, w), lambda bi, i: (bi, i, 0))
    lay = lambda a: _layer_spec(l, a.shape[1], a.shape[2])
    woa = pl.BlockSpec((None, SB_WIDTH, d), lambda bi, i: (l, 0, 0), pipeline_mode=pl.Buffered(1))
    wob = pl.BlockSpec((None, DN_WIDTH, d), lambda bi, i: (l, SB_WIDTH // DN_WIDTH, 0),
                       pipeline_mode=pl.Buffered(1))
    return pl.pallas_call(
        functools.partial(_post_kernel, ff_chunk=1024),
        grid=(b, t // tm),
        in_specs=[
            tok(d), tok(SB_WIDTH), tok(DN_WIDTH),
            pl.BlockSpec((None, None, N_MOD, d), lambda bi, i: (l, bi, 0, 0)),
            lay(gain), woa, wob, lay(w1), lay(w2),
        ],
        out_specs=tok(d),
        out_shape=jax.ShapeDtypeStruct((b, t, d), F32),
        compiler_params=pltpu.CompilerParams(
            dimension_semantics=("parallel", "parallel"), vmem_limit_bytes=VMEM_LIMIT),
        name="outproj_mlp",
    )(x, oa, ob, mod, gain, w_out, w_out, w1, w2)


def kernel(x, c, w_ada, b_ada, norm_mix, norm_mlp, w_in, sb_q_norm, sb_k_norm, conv_w, a_log, dt_bias,
           dn_out_norm, w_out, w_ff1, w_ff2):
    b, t, d = x.shape
    depth = w_ada.shape[0]
    tm = 512
    tq = 256
    tt = 512
    cum_rows = 256

    mod = _ada_call(c, w_ada, b_ada).reshape(depth, b, N_MOD, d)

    hi = jnp.arange(SB_WIDTH) // SB_HEAD_DIM
    blk = jnp.where(hi[:, None] == hi[None, :], 1.0 / SB_HEAD_DIM, 0.0).astype(BF16)
    ti = jnp.arange(cum_rows)
    cum = ((ti[:, None] // DN_CHUNK == ti[None, :] // DN_CHUNK) & (ti[None, :] <= ti[:, None])).astype(BF16)
    qi = jnp.arange(tq)
    u = (qi[:, None] > qi[None, :]).astype(BF16)

    ab0 = 3 * SB_WIDTH + 4 * DN_WIDTH
    n_ab = w_in.shape[2] - ab0
    w_in_b = w_in.astype(BF16)
    wab = jnp.pad(w_in[:, :, ab0:], ((0, 0), (0, 0), (0, AB_PAD - n_ab))).astype(BF16)
    w_out_b, w1_b, w2_b = w_out.astype(BF16), w_ff1.astype(BF16), w_ff2.astype(BF16)
    qg = jnp.tile(sb_q_norm, (1, SB_HEADS))[:, None, :]
    kg = jnp.tile(sb_k_norm, (1, SB_HEADS))[:, None, :]
    alog = jnp.pad(a_log, ((0, 0), (0, AB_PAD - DN_HEADS)))[:, None, :]
    dtb = jnp.pad(dt_bias, ((0, 0), (0, AB_PAD - DN_HEADS)))[:, None, :]
    gain_mix, gain_mlp = norm_mix[:, None, :], norm_mlp[:, None, :]

    for l in range(depth):
        q, k, v, dn, z, gbc, gbr = _premix_call(x, mod, l, gain_mix, w_in_b, wab, qg, kg, blk, alog, dtb, cum,
                                                conv_w, tm)
        oa = _sb_attn_call(q, k, v, u, tq)
        gbr5 = gbr.reshape(b, 2 * DN_HEADS, t // DN_CHUNK, 1, DN_CHUNK)
        ob = _gdn_call(dn, z, gbc, gbr5, dn_out_norm[l][None, :], tt)
        x = _post_call(x, oa, ob, mod, l, gain_mlp, w_out_b, w1_b, w2_b, tm)
    return x
```

```python
import functools

import jax
import jax.numpy as jnp
from jax import lax
from jax.experimental import pallas as pl
from jax.experimental.pallas import tpu as pltpu

F32 = jnp.float32
BF16 = jnp.bfloat16
EPS = 1e-6

SB_HEADS = 8
SB_HEAD_DIM = 64
SB_WIDTH = SB_HEADS * SB_HEAD_DIM
DN_HEADS = 4
DN_HEAD_DIM = 128
DN_WIDTH = DN_HEADS * DN_HEAD_DIM
DN_CHUNK = 64
CONV_WIDTH = 4
N_MOD = 6
LANES = 128
AB_PAD = LANES
VMEM_LIMIT = 56 * 1024 * 1024
LOG2E = 1.4426950408889634
SB_SKIP_LOG2 = 127.0

NN = (((1,), (0,)), ((), ()))
NT = (((1,), (1,)), ((), ()))
TN = (((0,), (0,)), ((), ()))


def _dot(a, b, dims=NN):
    return lax.dot_general(a, b, dims, preferred_element_type=F32)


def _split2(a):
    hi = a.astype(BF16)
    lo = (a - hi.astype(F32)).astype(BF16)
    return hi, lo


def _split3(a):
    hi = a.astype(BF16)
    r = a - hi.astype(F32)
    mid = r.astype(BF16)
    lo = (r - mid.astype(F32)).astype(BF16)
    return hi, mid, lo


def _dot_x3(a, b, dims=NN):
    ah, al = _split2(a)
    bh, bl = _split2(b)
    return _dot(ah, bh, dims) + (_dot(ah, bl, dims) + _dot(al, bh, dims))


def _dot_x1(a, b, dims=NN):
    return _dot(a.astype(BF16), b.astype(BF16), dims)


def _bmm(a, b, dims, dot):
    return jnp.stack([dot(a[g], b[g], dims) for g in range(a.shape[0])])


GDN_DOT_SCORES = _dot_x1
GDN_DOT_INVERSE = _dot_x1
GDN_DOT_SOLVE = _dot_x1
GDN_DOT_CHUNK = _dot_x1
GDN_DOT_STATE = _dot_x1
GDN_DOT_OUT = _dot_x1


def _sigmoid(x):
    return 1.0 / (1.0 + jnp.exp(-x))


def _silu(x):
    h = 0.5 * x
    return h + h * jnp.tanh(h)


def _softplus(x):
    return jnp.maximum(x, 0.0) + jnp.log1p(jnp.exp(-jnp.abs(x)))


def _ada_kernel(c_ref, w_ref, b_ref, o_ref):
    c = c_ref[...]
    cond = c * _sigmoid(c)
    o_ref[0] = _dot_x3(cond, w_ref[0]) + b_ref[0]


def _ada_call(c, w_ada, b_ada):
    depth, d, n = w_ada.shape
    b = c.shape[0]
    tn = 1536
    return pl.pallas_call(
        _ada_kernel,
        grid=(depth, n // tn),
        in_specs=[
            pl.BlockSpec((b, d), lambda l, j: (0, 0)),
            pl.BlockSpec((1, d, tn), lambda l, j: (l, 0, j)),
            pl.BlockSpec((1, 1, tn), lambda l, j: (l, 0, j)),
        ],
        out_specs=pl.BlockSpec((1, b, tn), lambda l, j: (l, 0, j)),
        out_shape=jax.ShapeDtypeStruct((depth, b, n), F32),
        compiler_params=pltpu.CompilerParams(
            dimension_semantics=("arbitrary", "arbitrary"), vmem_limit_bytes=VMEM_LIMIT),
        name="adaln_mod",
    )(c, w_ada, b_ada.reshape(depth, 1, n))


def _premix_kernel(x_ref, mod_ref, gain_ref, wsb_ref, wdn_ref, wz_ref, wab_ref, qg_ref, kg_ref,
                   blk_ref, alog_ref, dtb_ref, cum_ref, cw_ref,
                   q_ref, k_ref, v_ref, dn_ref, z_ref, gbc_ref, gbr_ref, xbuf):
    tm = x_ref.shape[1]
    x = x_ref[0]
    ms = jnp.mean(x * x, axis=-1, keepdims=True)
    y = x * lax.rsqrt(ms + EPS) * gain_ref[...]
    h = y * (1.0 + mod_ref[1:2, :]) + mod_ref[0:1, :]
    hb = h.astype(BF16)

    @pl.when(pl.program_id(1) == 0)
    def _():
        xbuf[0:8, :] = jnp.zeros((8, 3 * DN_WIDTH), F32)

    hd = DN_HEAD_DIM
    gw = 2 * hd
    n_groups = 3 * DN_WIDTH // gw

    def dn_group(j):
        xbuf[8:8 + tm, j * gw:(j + 1) * gw] = _dot(hb, wdn_ref[:, j * gw:(j + 1) * gw])

    dn_group(0)
    sb_groups = []
    ab = None
    for j in range(n_groups):
        if j + 1 < n_groups:
            dn_group(j + 1)
        sb_groups.append(_dot(hb, wsb_ref[:, j * gw:(j + 1) * gw]))
        if j < DN_WIDTH // gw:
            z_ref[0, :, j * gw:(j + 1) * gw] = _dot(hb, wz_ref[:, j * gw:(j + 1) * gw])
        elif ab is None:
            ab = _dot(hb, wab_ref[...])
        for cb in (2 * j, 2 * j + 1):
            ls = slice(cb * hd, (cb + 1) * hd)
            acc = cw_ref[CONV_WIDTH - 1:CONV_WIDTH, ls] * xbuf[8:8 + tm, ls]
            for kk in range(CONV_WIDTH - 1):
                off = 8 - (CONV_WIDTH - 1) + kk
                acc = acc + cw_ref[kk:kk + 1, ls] * xbuf[off:off + tm, ls]
            act = _silu(acc)
            if cb < 2 * DN_HEADS:
                act = act * lax.rsqrt(jnp.sum(act * act, axis=-1, keepdims=True) + EPS)
            if cb < DN_HEADS:
                act = act * (hd ** -0.5)
            dn_ref[0, :, ls] = act
    xbuf[0:8, :] = xbuf[tm:tm + 8, :]
    sb = jnp.concatenate(sb_groups, axis=-1)

    blk = blk_ref[...]

    def head_rmsnorm(a, g):
        m = _dot((a * a).astype(BF16), blk)
        return a * lax.rsqrt(m + EPS) * g

    q = head_rmsnorm(sb[:, :SB_WIDTH], qg_ref[...]) * (SB_HEAD_DIM ** -0.5 * LOG2E)
    k = head_rmsnorm(sb[:, SB_WIDTH:2 * SB_WIDTH], kg_ref[...])
    q_ref[0] = q.astype(BF16)
    k_ref[0] = k.astype(BF16)
    v_ref[0] = sb[:, 2 * SB_WIDTH:].astype(BF16)

    lane =lax.broadcasted_iota(jnp.int32, ab.shape, 1)
    g = -jnp.exp(alog_ref[...]) * _softplus(ab + dtb_ref[...])
    cum = cum_ref[...]
    rows = cum.shape[0]
    gcs = []
    for r0 in range(0, g.shape[0], rows):
        g1, g2, g3 = _split3(g[r0:r0 + rows])
        gcs.append(_dot(cum, g1) + (_dot(cum, g2) + _dot(cum, g3)))
    gc = jnp.concatenate(gcs, axis=0)
    gb = jnp.where(lane < DN_HEADS, gc, _sigmoid(ab))
    gbc_ref[0] = gb
    gbr_ref[0] = gb.T[0:2 * DN_HEADS, :]


def _layer_spec(l, rows, cols, col_block=0):
    return pl.BlockSpec((None, rows, cols), lambda bi, i: (l, 0, col_block), pipeline_mode=pl.Buffered(1))


def _premix_call(x, mod, l, gain, w_in, wab, qg, kg, blk, alog, dtb, cum, conv_w, tm):
    b, t, d = x.shape
    full = lambda a: pl.BlockSpec(a.shape, lambda bi, i: (0,) * a.ndim, pipeline_mode=pl.Buffered(1))
    tok = lambda w: pl.BlockSpec((1, tm, w), lambda bi, i: (bi, i, 0))
    lay = lambda a: _layer_spec(l, a.shape[1], a.shape[2])
    sbw, dnw = 3 * SB_WIDTH, 3 * DN_WIDTH
    assert sbw == dnw and (sbw + dnw) % DN_WIDTH == 0
    return pl.pallas_call(
        _premix_kernel,
        grid=(b, t // tm),
        in_specs=[
            tok(d),
            pl.BlockSpec((None, None, N_MOD, d), lambda bi, i: (l, bi, 0, 0)),
            lay(gain), _layer_spec(l, d, sbw, 0), _layer_spec(l, d, dnw, 1),
            _layer_spec(l, d, DN_WIDTH, (sbw + dnw) // DN_WIDTH), lay(wab), lay(qg), lay(kg), full(blk),
            lay(alog), lay(dtb), full(cum), lay(conv_w),
        ],
        out_specs=[tok(SB_WIDTH), tok(SB_WIDTH), tok(SB_WIDTH), tok(3 * DN_WIDTH), tok(DN_WIDTH),
                   tok(AB_PAD),
                   pl.BlockSpec((1, 2 * DN_HEADS, tm), lambda bi, i: (bi, 0, i))],
        out_shape=[
            jax.ShapeDtypeStruct((b, t, SB_WIDTH), BF16),
            jax.ShapeDtypeStruct((b, t, SB_WIDTH), BF16),
            jax.ShapeDtypeStruct((b, t, SB_WIDTH), BF16),
            jax.ShapeDtypeStruct((b, t, 3 * DN_WIDTH), F32),
            jax.ShapeDtypeStruct((b, t, DN_WIDTH), F32),
            jax.ShapeDtypeStruct((b, t, AB_PAD), F32),
            jax.ShapeDtypeStruct((b, 2 * DN_HEADS, t), F32),
        ],
        scratch_shapes=[pltpu.VMEM((tm + 8, 3 * DN_WIDTH), F32)],
        compiler_params=pltpu.CompilerParams(
            dimension_semantics=("parallel", "arbitrary"), vmem_limit_bytes=VMEM_LIMIT),
        name="premix_proj",
    )(x, mod, gain, w_in, w_in, w_in, wab, qg, kg, blk, alog, dtb, cum, conv_w)


SB_PAIRS = 4


def _sb_softplus2(z):
    return jnp.maximum(z, 0.0) + jnp.log(1.0 + jnp.exp2(-jnp.abs(z))) * LOG2E


def _sb_attn_kernel(q_ref, k_ref, v_ref, u_ref, o_ref, *, tq):
    i = pl.program_id(2)
    lane = lax.broadcasted_iota(jnp.int32, (1, LANES), 1)
    first = lane < SB_HEAD_DIM
    q2s, lanes = [], []
    for p in range(SB_PAIRS):
        q = q_ref[0, :, p * LANES:(p + 1) * LANES]
        zero = jnp.zeros_like(q)
        q2s.append(jnp.concatenate([jnp.where(first, q, zero), jnp.where(first, zero, q)], axis=0))
        lanes.append(slice(p * LANES, (p + 1) * LANES))
    n_heads = 2 * SB_PAIRS
    u = u_ref[...]
    hq = tq // 2
    tri = (lax.broadcasted_iota(jnp.int32, (hq, hq), 1)
           < lax.broadcasted_iota(jnp.int32, (hq, hq), 0))
    zeros_q = jnp.zeros((hq, hq), BF16)

    def quadrants(a):
        return a[:hq, :hq], a[hq:, :hq], a[hq:, hq:]

    def diag_tile(tl, bl, br):
        tl = jnp.where(tri, tl, 0.0).astype(BF16)
        br = jnp.where(tri, br, 0.0).astype(BF16)
        return jnp.concatenate([jnp.concatenate([tl, zeros_q], axis=1),
                                jnp.concatenate([bl.astype(BF16), br], axis=1)], axis=0)

    def kv_block(j, ls):
        start = pl.multiple_of(j * tq, tq)
        return k_ref[0, pl.ds(start, tq), ls], v_ref[0, pl.ds(start, tq), ls]

    def scores(j):
        zs, vbs = [], []
        for p in range(SB_PAIRS):
            kb, vb = kv_block(j, lanes[p])
            z2 = _dot(q2s[p], kb, NT)
            zs += [z2[:tq], z2[tq:]]
            vbs.append(vb)
        return zs, vbs

    def suffix_sums(spbs):
        suf = _dot(jnp.concatenate(spbs, axis=0), u)
        sufs = [suf[n * tq:(n + 1) * tq] for n in range(len(spbs))]
        tots = [sf[:, 0:1] + spb[:, 0:1].astype(F32) for sf, spb in zip(sufs, spbs)]
        return sufs, tots

    def weighted_values(atts, vbs):
        pvs = []
        for p in range(SB_PAIRS):
            pv2 = _dot(jnp.concatenate(atts[2 * p:2 * p + 2], axis=0), vbs[p])
            pvs += [pv2[:tq], pv2[tq:]]
        return pvs

    has_prev = jnp.where(i > 0, 1.0, 0.0)
    z_d, v_d = scores(i)
    z_p, v_p = scores(jnp.maximum(i - 1, 0))
    zq_d = [quadrants(z) for z in z_d]
    spq_d = [[_sb_softplus2(zz) for zz in zq] for zq in zq_d]
    lbq_d = [[zz - sp for zz, sp in zip(zq, spq)] for zq, spq in zip(zq_d, spq_d)]
    sp_p = [_sb_softplus2(z) for z in z_p]
    lb_p = [z - sp for z, sp in zip(z_p, sp_p)]
    sufs, tots = suffix_sums([diag_tile(*spq) for spq in spq_d] + [sp.astype(BF16) for sp in sp_p])
    suf_d, suf_p = sufs[:n_heads], sufs[n_heads:]
    tot_d, tot_p = tots[:n_heads], tots[n_heads:]
    att_d = [diag_tile(*[jnp.exp2(lb - sf) for lb, sf in zip(lbq, quadrants(suf))])
             for lbq, suf in zip(lbq_d, suf_d)]
    att_p = [jnp.exp2(lb - suf - tot).astype(BF16) for lb, suf, tot in zip(lb_p, suf_p, tot_d)]
    pv_d = weighted_values(att_d, v_d)
    pv_p = weighted_values(att_p, v_p)
    accs = tuple(d + has_prev * p for d, p in zip(pv_d, pv_p))
    cs = tuple(d + has_prev * p for d, p in zip(tot_d, tot_p))

    def cond(st):
        j, _, c = st
        cmin = c[0]
        for cc in c[1:]:
            cmin = jnp.minimum(cmin, cc)
        return jnp.logical_and(j >= 0, jnp.min(cmin) < SB_SKIP_LOG2)

    def body(st):
        j, a, c = st
        zs, vbs = scores(j)
        sps = [_sb_softplus2(z) for z in zs]
        sufs, tots = suffix_sums([sp.astype(BF16) for sp in sps])
        atts = [jnp.exp2((z - sp) - suf - cc).astype(BF16) for z, sp, suf, cc in zip(zs, sps, sufs, c)]
        pvs = weighted_values(atts, vbs)
        return (j - 1, tuple(aa + pv for aa, pv in zip(a, pvs)),
                tuple(cc + tot for cc, tot in zip(c, tots)))

    _, accs, _ = lax.while_loop(cond, body, (i - 2, accs, cs))
    for p in range(SB_PAIRS):
        o_ref[0, :, p * LANES:(p + 1) * LANES] = jnp.where(first, accs[2 * p], accs[2 * p + 1]).astype(BF16)


def _sb_attn_call(q, k, v, u, tq):
    b, t, _ = q.shape
    w = SB_PAIRS * LANES
    return pl.pallas_call(
        functools.partial(_sb_attn_kernel, tq=tq),
        grid=(b, SB_WIDTH // w, t // tq),
        in_specs=[
            pl.BlockSpec((1, tq, w), lambda bi, p, i: (bi, i, p)),
            pl.BlockSpec((1, t, w), lambda bi, p, i: (bi, 0, p)),
            pl.BlockSpec((1, t, w), lambda bi, p, i: (bi, 0, p)),
            pl.BlockSpec((tq, tq), lambda bi, p, i: (0, 0)),
        ],
        out_specs=pl.BlockSpec((1, tq, w), lambda bi, p, i: (bi, i, p)),
        out_shape=jax.ShapeDtypeStruct((b, t, SB_WIDTH), BF16),
        compiler_params=pltpu.CompilerParams(
            dimension_semantics=("parallel", "parallel", "arbitrary"), vmem_limit_bytes=VMEM_LIMIT),
        name="sb_attention",
    )(q, k, v, u)


def _unit_lower_inverse(a, ri, ci):
    mm = lambda p_, q_: _bmm(p_, q_, NN, GDN_DOT_INVERSE)
    same = ri == ci
    x = None
    for level in range(1, 7):
        same_next = (ri >> level) == (ci >> level)
        off = jnp.where(jnp.logical_and(same_next, jnp.logical_not(same)), a, 0.0)
        if x is None:
            x = jnp.where(ri == ci, 1.0, 0.0) - off
        else:
            x = x - mm(x, mm(off, x))
        same = same_next
    return x


def _gdn_kernel(qkv_ref, z_ref, gbc_ref, gbr_ref, on_ref, o_ref, state, *, tt):
    c_sz = DN_CHUNK
    hd = DN_HEAD_DIM
    nc = tt // c_sz
    qkv_s = qkv_ref.at[0]

    @pl.when(pl.program_id(1) == 0)
    def _():
        state[...] = jnp.zeros_like(state)

    ri = lax.broadcasted_iota(jnp.int32, (c_sz, c_sz), 0)
    ci = lax.broadcasted_iota(jnp.int32, (c_sz, c_sz), 1)
    tril_incl = ci <= ri
    tril_strict = ci < ri

    gcol = gbc_ref[0].reshape(nc, c_sz, AB_PAD)
    heads = range(DN_HEADS)

    def cat(f):
        return jnp.concatenate([f(h) for h in heads], axis=0)

    q = cat(lambda h: qkv_s[:, h * hd:(h + 1) * hd].reshape(nc, c_sz, hd))
    k = cat(lambda h: qkv_s[:, DN_WIDTH + h * hd:DN_WIDTH + (h + 1) * hd].reshape(nc, c_sz, hd))
    v = cat(lambda h: qkv_s[:, 2 * DN_WIDTH + h * hd:2 * DN_WIDTH + (h + 1) * hd].reshape(nc, c_sz, hd))
    gc_col = cat(lambda h: gcol[:, :, h:h + 1])
    beta_col = cat(lambda h: gcol[:, :, DN_HEADS + h:DN_HEADS + h + 1])
    gc_row = gbr_ref[0, 0:DN_HEADS].reshape(DN_HEADS * nc, 1, c_sz)
    g_last = gc_row[:, :, c_sz - 1:c_sz]
    decay = jnp.where(tril_incl, jnp.exp(jnp.where(tril_incl, gc_col - gc_row, 0.0)), 0.0)
    k_beta = k * beta_col
    a = jnp.where(tril_strict, _bmm(k_beta, k, NT, GDN_DOT_SCORES) * decay, 0.0)
    t_mat = _unit_lower_inverse(a, ri, ci)
    rhs = jnp.concatenate([v * beta_col, k_beta * jnp.exp(gc_col)], axis=-1)
    uw = _bmm(t_mat, rhs, NN, GDN_DOT_SOLVE)
    intra = jnp.where(tril_incl, _bmm(q, k, NT, GDN_DOT_SCORES) * decay, 0.0)
    kd = k * jnp.exp(g_last - gc_col)
    iuw = _bmm(intra, uw, NN, GDN_DOT_CHUNK)
    kuw = _bmm(kd, uw, TN, GDN_DOT_CHUNK)
    q_eff = q * jnp.exp(gc_col) - iuw[:, :, hd:]
    o_base = iuw[:, :, :hd]
    e_last = jnp.exp(g_last)
    kw = kuw[:, :, hd:]
    n_mat = kuw[:, :, :hd]

    s_cur = [state[h] for h in heads]
    s_before = [[None] * nc for _ in heads]
    for c in range(nc):
        for h in heads:
            g = h * nc + c
            s_before[h][c] = s_cur[h]
            s_cur[h] = (e_last[g] * s_cur[h] - GDN_DOT_STATE(kw[g], s_cur[h])) + n_mat[g]
    for h in heads:
        state[h] = s_cur[h]
    s_all = jnp.stack([s_before[h][c] for h in heads for c in range(nc)])
    o_all = _bmm(q_eff, s_all, NN, GDN_DOT_OUT) + o_base
    outs = []
    for h in heads:
        o = o_all[h * nc:(h + 1) * nc].reshape(tt, hd)
        ms = jnp.mean(o * o, axis=-1, keepdims=True)
        zz = z_ref[0, :, h * hd:(h + 1) * hd]
        outs.append((o * lax.rsqrt(ms + EPS) * on_ref[...] * _silu(zz)).astype(BF16))
    o_ref[0] = jnp.concatenate(outs, axis=-1)


def _gdn_call(dn, z, gbc, gbr5, out_norm, tt):
    b, t, _ = dn.shape
    nc = tt // DN_CHUNK
    return pl.pallas_call(
        functools.partial(_gdn_kernel, tt=tt),
        grid=(b, t // tt),
        in_specs=[
            pl.BlockSpec((1, tt, 3 * DN_WIDTH), lambda bi, i: (bi, i, 0)),
            pl.BlockSpec((1, tt, DN_WIDTH), lambda bi, i: (bi, i, 0)),
            pl.BlockSpec((1, tt, AB_PAD), lambda bi, i: (bi, i, 0)),
            pl.BlockSpec((1, 2 * DN_HEADS, nc, 1, DN_CHUNK), lambda bi, i: (bi, 0, i, 0, 0)),
            pl.BlockSpec(out_norm.shape, lambda bi, i: (0, 0)),
        ],
        out_specs=pl.BlockSpec((1, tt, DN_WIDTH), lambda bi, i: (bi, i, 0)),
        out_shape=jax.ShapeDtypeStruct((b, t, DN_WIDTH), BF16),
        scratch_shapes=[
            pltpu.VMEM((DN_HEADS, DN_HEAD_DIM, DN_HEAD_DIM), F32),
        ],
        compiler_params=pltpu.CompilerParams(
            dimension_semantics=("parallel", "arbitrary"), vmem_limit_bytes=VMEM_LIMIT),
        name="gated_deltanet",
    )(dn, z, gbc, gbr5, out_norm)


def _post_kernel(x_ref, oa_ref, ob_ref, mod_ref, gain_ref, woa_ref, wob_ref, w1_ref, w2_ref, o_ref, *, ff_chunk):
    mix = _dot(oa_ref[0], woa_ref[...]) + _dot(ob_ref[0], wob_ref[...])
    x1 = x_ref[0] + mod_ref[2:3, :] * mix
    ms = jnp.mean(x1 * x1, axis=-1, keepdims=True)
    y = x1 * lax.rsqrt(ms + EPS) * gain_ref[...]
    hb = (y * (1.0 + mod_ref[4:5, :]) + mod_ref[3:4, :]).astype(BF16)
    d_ff = w1_ref.shape[1]
    acc = jnp.zeros(x1.shape, F32)
    for f in range(d_ff // ff_chunk):
        fs = slice(f * ff_chunk, (f + 1) * ff_chunk)
        hid = jnp.maximum(_dot(hb, w1_ref[:, fs]), 0.0)
        acc = acc + _dot((hid * hid).astype(BF16), w2_ref[fs, :])
    o_ref[0] = x1 + mod_ref[5:6, :] * acc


def _post_call(x, oa, ob, mod, l, gain, w_out, w1, w2, tm):
    b, t, d = x.shape
    tok = lambda w: pl.BlockSpec((1, tm, w), lambda bi, i: (bi, i, 0))
    lay = lambda a: _layer_spec(l, a.shape[1], a.shape[2])
    woa = pl.BlockSpec((None, SB_WIDTH, d), lambda bi, i: (l, 0, 0), pipeline_mode=pl.Buffered(1))
    wob = pl.BlockSpec((None, DN_WIDTH, d), lambda bi, i: (l, SB_WIDTH // DN_WIDTH, 0),
                       pipeline_mode=pl.Buffered(1))
    return pl.pallas_call(
        functools.partial(_post_kernel, ff_chunk=1024),
        grid=(b, t // tm),
        in_specs=[
            tok(d), tok(SB_WIDTH), tok(DN_WIDTH),
            pl.BlockSpec((None, None, N_MOD, d), lambda bi, i: (l, bi, 0, 0)),
            lay(gain), woa, wob, lay(w1), lay(w2),
        ],
        out_specs=tok(d),
        out_shape=jax.ShapeDtypeStruct((b, t, d), F32),
        compiler_params=pltpu.CompilerParams(
            dimension_semantics=("parallel", "parallel"), vmem_limit_bytes=VMEM_LIMIT),
        name="outproj_mlp",
    )(x, oa, ob, mod, gain, w_out, w_out, w1, w2)


def kernel(x, c, w_ada, b_ada, norm_mix, norm_mlp, w_in, sb_q_norm, sb_k_norm, conv_w, a_log, dt_bias,
           dn_out_norm, w_out, w_ff1, w_ff2):
    b, t, d = x.shape
    depth = w_ada.shape[0]
    tm = 512
    tq = 256
    tt = 512
    cum_rows = 256

    mod = _ada_call(c, w_ada, b_ada).reshape(depth, b, N_MOD, d)

    hi = jnp.arange(SB_WIDTH) // SB_HEAD_DIM
    blk = jnp.where(hi[:, None] == hi[None, :], 1.0 / SB_HEAD_DIM, 0.0).astype(BF16)
    ti = jnp.arange(cum_rows)
    cum = ((ti[:, None] // DN_CHUNK == ti[None, :] // DN_CHUNK) & (ti[None, :] <= ti[:, None])).astype(BF16)
    qi = jnp.arange(tq)
    u = (qi[:, None] > qi[None, :]).astype(BF16)

    ab0 = 3 * SB_WIDTH + 4 * DN_WIDTH
    n_ab = w_in.shape[2] - ab0
    w_in_b = w_in[:, :, :ab0].astype(BF16)
    wab = jnp.pad(w_in[:, :, ab0:], ((0, 0), (0, 0), (0, AB_PAD - n_ab))).astype(BF16)
    w_out_b, w1_b, w2_b = w_out.astype(BF16), w_ff1.astype(BF16), w_ff2.astype(BF16)
    qg = jnp.tile(sb_q_norm, (1, SB_HEADS))[:, None, :]
    kg = jnp.tile(sb_k_norm, (1, SB_HEADS))[:, None, :]
    alog = jnp.pad(a_log, ((0, 0), (0, AB_PAD - DN_HEADS)))[:, None, :]
    dtb = jnp.pad(dt_bias, ((0, 0), (0, AB_PAD - DN_HEADS)))[:, None, :]
    gain_mix, gain_mlp = norm_mix[:, None, :], norm_mlp[:, None, :]

    for l in range(depth):
        q, k, v, dn, z, gbc, gbr = _premix_call(x, mod, l, gain_mix, w_in_b, wab, qg, kg, blk, alog, dtb, cum,
                                                conv_w, tm)
        oa = _sb_attn_call(q, k, v, u, tq)
        gbr5 = gbr.reshape(b, 2 * DN_HEADS, t // DN_CHUNK, 1, DN_CHUNK)
        ob = _gdn_call(dn, z, gbc, gbr5, dn_out_norm[l][None, :], tt)
        x = _post_call(x, oa, ob, mod, l, gain_mlp, w_out_b, w1_b, w2_b, tm)
    return x
```

```python
import functools

import jax
import jax.numpy as jnp
from jax import lax
from jax.experimental import pallas as pl
from jax.experimental.pallas import tpu as pltpu

F32 = jnp.float32
BF16 = jnp.bfloat16
EPS = 1e-6

SB_HEADS = 8
SB_HEAD_DIM = 64
SB_WIDTH = SB_HEADS * SB_HEAD_DIM
DN_HEADS = 4
DN_HEAD_DIM = 128
DN_WIDTH = DN_HEADS * DN_HEAD_DIM
DN_CHUNK = 64
CONV_WIDTH = 4
N_MOD = 6
LANES = 128
AB_PAD = LANES
VMEM_LIMIT = 56 * 1024 * 1024
LOG2E = 1.4426950408889634
SB_SKIP_LOG2 = 127.0

NN = (((1,), (0,)), ((), ()))
NT = (((1,), (1,)), ((), ()))
TN = (((0,), (0,)), ((), ()))


def _dot(a, b, dims=NN):
    return lax.dot_general(a, b, dims, preferred_element_type=F32)


def _split2(a):
    hi = a.astype(BF16)
    lo = (a - hi.astype(F32)).astype(BF16)
    return hi, lo


def _split3(a):
    hi = a.astype(BF16)
    r = a - hi.astype(F32)
    mid = r.astype(BF16)
    lo = (r - mid.astype(F32)).astype(BF16)
    return hi, mid, lo


def _dot_x3(a, b, dims=NN):
    ah, al = _split2(a)
    bh, bl = _split2(b)
    return _dot(ah, bh, dims) + (_dot(ah, bl, dims) + _dot(al, bh, dims))


def _dot_x1(a, b, dims=NN):
    return _dot(a.astype(BF16), b.astype(BF16), dims)


def _bmm(a, b, dims, dot):
    return jnp.stack([dot(a[g], b[g], dims) for g in range(a.shape[0])])


GDN_DOT_SCORES = _dot_x1
GDN_DOT_INVERSE = _dot_x1
GDN_DOT_SOLVE = _dot_x1
GDN_DOT_CHUNK = _dot_x1
GDN_DOT_STATE = _dot_x1
GDN_DOT_OUT = _dot_x1


def _sigmoid(x):
    return 1.0 / (1.0 + jnp.exp(-x))


def _silu(x):
    h = 0.5 * x
    return h + h * jnp.tanh(h)


def _softplus(x):
    return jnp.maximum(x, 0.0) + jnp.log1p(jnp.exp(-jnp.abs(x)))


def _ada_kernel(c_ref, w_ref, b_ref, o_ref):
    c = c_ref[...]
    cond = c * _sigmoid(c)
    o_ref[0] = _dot_x3(cond, w_ref[0]) + b_ref[0]


def _ada_call(c, w_ada, b_ada):
    depth, d, n = w_ada.shape
    b = c.shape[0]
    tn = 1536
    return pl.pallas_call(
        _ada_kernel,
        grid=(depth, n // tn),
        in_specs=[
            pl.BlockSpec((b, d), lambda l, j: (0, 0)),
            pl.BlockSpec((1, d, tn), lambda l, j: (l, 0, j)),
            pl.BlockSpec((1, 1, tn), lambda l, j: (l, 0, j)),
        ],
        out_specs=pl.BlockSpec((1, b, tn), lambda l, j: (l, 0, j)),
        out_shape=jax.ShapeDtypeStruct((depth, b, n), F32),
        compiler_params=pltpu.CompilerParams(
            dimension_semantics=("arbitrary", "arbitrary"), vmem_limit_bytes=VMEM_LIMIT),
        name="adaln_mod",
    )(c, w_ada, b_ada.reshape(depth, 1, n))


def _premix_kernel(x_ref, mod_ref, gain_ref, wsb_ref, wdn_ref, wz_ref, wab_ref, qg_ref, kg_ref,
                   blk_ref, alog_ref, dtb_ref, cum_ref, cw_ref,
                   q_ref, k_ref, v_ref, dn_ref, z_ref, gbc_ref, gbr_ref, xbuf):
    tm = x_ref.shape[1]
    x = x_ref[0]
    ms = jnp.mean(x * x, axis=-1, keepdims=True)
    y = x * lax.rsqrt(ms + EPS) * gain_ref[...]
    h = y * (1.0 + mod_ref[1:2, :]) + mod_ref[0:1, :]
    hb = h.astype(BF16)

    @pl.when(pl.program_id(1) == 0)
    def _():
        xbuf[0:8, :] = jnp.zeros((8, 3 * DN_WIDTH), F32)

    hd = DN_HEAD_DIM
    gw = 2 * hd
    n_groups = 3 * DN_WIDTH // gw

    def dn_group(j):
        xbuf[8:8 + tm, j * gw:(j + 1) * gw] = _dot(hb, wdn_ref[:, j * gw:(j + 1) * gw])

    dn_group(0)
    sb_groups = []
    ab = None
    for j in range(n_groups):
        if j + 1 < n_groups:
            dn_group(j + 1)
        sb_groups.append(_dot(hb, wsb_ref[:, j * gw:(j + 1) * gw]))
        if j < DN_WIDTH // gw:
            z_ref[0, :, j * gw:(j + 1) * gw] = _dot(hb, wz_ref[:, j * gw:(j + 1) * gw])
        elif ab is None:
            ab = _dot(hb, wab_ref[...])
        for cb in (2 * j, 2 * j + 1):
            ls = slice(cb * hd, (cb + 1) * hd)
            acc = cw_ref[CONV_WIDTH - 1:CONV_WIDTH, ls] * xbuf[8:8 + tm, ls]
            for kk in range(CONV_WIDTH - 1):
                off = 8 - (CONV_WIDTH - 1) + kk
                acc = acc + cw_ref[kk:kk + 1, ls] * xbuf[off:off + tm, ls]
            act = _silu(acc)
            if cb < 2 * DN_HEADS:
                act = act * lax.rsqrt(jnp.sum(act * act, axis=-1, keepdims=True) + EPS)
            if cb < DN_HEADS:
                act = act * (hd ** -0.5)
            dn_ref[0, :, ls] = act
    xbuf[0:8, :] = xbuf[tm:tm + 8, :]
    sb = jnp.concatenate(sb_groups, axis=-1)

    blk = blk_ref[...]

    def head_rmsnorm(a, g):
        m = _dot((a * a).astype(BF16), blk)
        return a * lax.rsqrt(m + EPS) * g

    q = head_rmsnorm(sb[:, :SB_WIDTH], qg_ref[...]) * (SB_HEAD_DIM ** -0.5 * LOG2E)
    k = head_rmsnorm(sb[:, SB_WIDTH:2 * SB_WIDTH], kg_ref[...])
    q_ref[0] = q.astype(BF16)
    k_ref[0] = k.astype(BF16)
    v_ref[0] = sb[:, 2 * SB_WIDTH:].astype(BF16)

    lane =lax.broadcasted_iota(jnp.int32, ab.shape, 1)
    g = -jnp.exp(alog_ref[...]) * _softplus(ab + dtb_ref[...])
    cum = cum_ref[...]
    rows = cum.shape[0]
    gcs = []
    for r0 in range(0, g.shape[0], rows):
        g1, g2, g3 = _split3(g[r0:r0 + rows])
        gcs.append(_dot(cum, g1) + (_dot(cum, g2) + _dot(cum, g3)))
    gc = jnp.concatenate(gcs, axis=0)
    gb = jnp.where(lane < DN_HEADS, gc, _sigmoid(ab))
    gbc_ref[0] = gb
    gbr_ref[0] = gb.T[0:2 * DN_HEADS, :]


def _layer_spec(l, rows, cols, col_block=0):
    return pl.BlockSpec((None, rows, cols), lambda bi, i: (l, 0, col_block), pipeline_mode=pl.Buffered(1))


def _premix_call(x, mod, l, gain, w_in, wab, qg, kg, blk, alog, dtb, cum, conv_w, tm):
    b, t, d = x.shape
    full = lambda a: pl.BlockSpec(a.shape, lambda bi, i: (0,) * a.ndim, pipeline_mode=pl.Buffered(1))
    tok = lambda w: pl.BlockSpec((1, tm, w), lambda bi, i: (bi, i, 0))
    lay = lambda a: _layer_spec(l, a.shape[1], a.shape[2])
    sbw, dnw = 3 * SB_WIDTH, 3 * DN_WIDTH
    assert sbw == dnw and (sbw + dnw) % DN_WIDTH == 0
    return pl.pallas_call(
        _premix_kernel,
        grid=(b, t // tm),
        in_specs=[
            tok(d),
            pl.BlockSpec((None, None, N_MOD, d), lambda bi, i: (l, bi, 0, 0)),
            lay(gain), _layer_spec(l, d, sbw, 0), _layer_spec(l, d, dnw, 1),
            _layer_spec(l, d, DN_WIDTH, (sbw + dnw) // DN_WIDTH), lay(wab), lay(qg), lay(kg), full(blk),
            lay(alog), lay(dtb), full(cum), lay(conv_w),
        ],
        out_specs=[tok(SB_WIDTH), tok(SB_WIDTH), tok(SB_WIDTH), tok(3 * DN_WIDTH), tok(DN_WIDTH),
                   tok(AB_PAD),
                   pl.BlockSpec((1, 2 * DN_HEADS, tm), lambda bi, i: (bi, 0, i))],
        out_shape=[
            jax.ShapeDtypeStruct((b, t, SB_WIDTH), BF16),
            jax.ShapeDtypeStruct((b, t, SB_WIDTH), BF16),
            jax.ShapeDtypeStruct((b, t, SB_WIDTH), BF16),
            jax.ShapeDtypeStruct((b, t, 3 * DN_WIDTH), F32),
            jax.ShapeDtypeStruct((b, t, DN_WIDTH), F32),
            jax.ShapeDtypeStruct((b, t, AB_PAD), F32),
            jax.ShapeDtypeStruct((b, 2 * DN_HEADS, t), F32),
        ],
        scratch_shapes=[pltpu.VMEM((tm + 8, 3 * DN_WIDTH), F32)],
        compiler_params=pltpu.CompilerParams(
            dimension_semantics=("parallel", "arbitrary"), vmem_limit_bytes=VMEM_LIMIT),
        name="premix_proj",
    )(x, mod, gain, w_in, w_in, w_in, wab, qg, kg, blk, alog, dtb, cum, conv_w)


SB_PAIRS = 4


def _sb_softplus2(z):
    return jnp.maximum(z, 0.0) + jnp.log(1.0 + jnp.exp2(-jnp.abs(z))) * LOG2E


def _sb_attn_kernel(q_ref, k_ref, v_ref, u_ref, o_ref, *, tq):
    i = pl.program_id(2)
    lane = lax.broadcasted_iota(jnp.int32, (1, LANES), 1)
    first = lane < SB_HEAD_DIM
    q2s, lanes = [], []
    for p in range(SB_PAIRS):
        q = q_ref[0, :, p * LANES:(p + 1) * LANES]
        zero = jnp.zeros_like(q)
        q2s.append(jnp.concatenate([jnp.where(first, q, zero), jnp.where(first, zero, q)], axis=0))
        lanes.append(slice(p * LANES, (p + 1) * LANES))
    n_heads = 2 * SB_PAIRS
    u = u_ref[...]
    hq = tq // 2
    tri = (lax.broadcasted_iota(jnp.int32, (hq, hq), 1)
           < lax.broadcasted_iota(jnp.int32, (hq, hq), 0))
    zeros_q = jnp.zeros((hq, hq), BF16)

    def quadrants(a):
        return a[:hq, :hq], a[hq:, :hq], a[hq:, hq:]

    def diag_tile(tl, bl, br):
        tl = jnp.where(tri, tl, 0.0).astype(BF16)
        br = jnp.where(tri, br, 0.0).astype(BF16)
        return jnp.concatenate([jnp.concatenate([tl, zeros_q], axis=1),
                                jnp.concatenate([bl.astype(BF16), br], axis=1)], axis=0)

    def kv_block(j, ls):
        start = pl.multiple_of(j * tq, tq)
        return k_ref[0, pl.ds(start, tq), ls], v_ref[0, pl.ds(start, tq), ls]

    def scores(j):
        zs, vbs = [], []
        for p in range(SB_PAIRS):
            kb, vb = kv_block(j, lanes[p])
            z2 = _dot(q2s[p], kb, NT)
            zs += [z2[:tq], z2[tq:]]
            vbs.append(vb)
        return zs, vbs

    def suffix_sums(spbs):
        suf = _dot(jnp.concatenate(spbs, axis=0), u)
        sufs, r0 = [], 0
        for spb in spbs:
            sufs.append(suf[r0:r0 + spb.shape[0]])
            r0 += spb.shape[0]
        tots = [sf[:, 0:1] + spb[:, 0:1].astype(F32) for sf, spb in zip(sufs, spbs)]
        return sufs, tots

    def weighted_values(atts, vbs):
        pvs = []
        for p in range(SB_PAIRS):
            rows = atts[2 * p].shape[0]
            pv2 = _dot(jnp.concatenate(atts[2 * p:2 * p + 2], axis=0), vbs[p])
            pvs += [pv2[:rows], pv2[rows:]]
        return pvs

    def prev_rows(zs, carries, vbs):
        sps = [_sb_softplus2(z) for z in zs]
        sufs, tots = suffix_sums([sp.astype(BF16) for sp in sps])
        atts = [jnp.exp2((z - sp) - suf - cc).astype(BF16) for z, sp, suf, cc in zip(zs, sps, sufs, carries)]
        return weighted_values(atts, vbs), tots

    has_prev = jnp.where(i > 0, 1.0, 0.0)
    z_d, v_d = scores(i)
    z_p, v_p = scores(jnp.maximum(i - 1, 0))
    zq_d = [quadrants(z) for z in z_d]
    spq_d = [[_sb_softplus2(zz) for zz in zq] for zq in zq_d]
    lbq_d = [[zz - sp for zz, sp in zip(zq, spq)] for zq, spq in zip(zq_d, spq_d)]
    sp_p = [_sb_softplus2(z) for z in z_p]
    lb_p = [z - sp for z, sp in zip(z_p, sp_p)]
    sufs, tots = suffix_sums([diag_tile(*spq) for spq in spq_d] + [sp.astype(BF16) for sp in sp_p])
    suf_d, suf_p = sufs[:n_heads], sufs[n_heads:]
    tot_d, tot_p = tots[:n_heads], tots[n_heads:]
    att_d = [diag_tile(*[jnp.exp2(lb - sf) for lb, sf in zip(lbq, quadrants(suf))])
             for lbq, suf in zip(lbq_d, suf_d)]
    att_p = [jnp.exp2(lb - suf - tot).astype(BF16) for lb, suf, tot in zip(lb_p, suf_p, tot_d)]
    pv_d = weighted_values(att_d, v_d)
    pv_p = weighted_values(att_p, v_p)
    accs = tuple(d + has_prev * p for d, p in zip(pv_d, pv_p))
    cs = tuple(d + has_prev * p for d, p in zip(tot_d, tot_p))

    def cond(st):
        j, _, c = st
        cmin = c[0]
        for cc in c[1:]:
            cmin = jnp.minimum(cmin, cc)
        return jnp.logical_and(j >= 0, jnp.min(cmin) < SB_SKIP_LOG2)

    def body(st):
        j, a, c = st
        zs, vbs = scores(j)
        pvs, tots = prev_rows(zs, c, vbs)
        return (j - 1, tuple(aa + pv for aa, pv in zip(a, pvs)),
                tuple(cc + tot for cc, tot in zip(c, tots)))

    _, accs, _ = lax.while_loop(cond, body, (i - 2, accs, cs))
    for p in range(SB_PAIRS):
        o_ref[0, :, p * LANES:(p + 1) * LANES] = jnp.where(first, accs[2 * p], accs[2 * p + 1]).astype(BF16)


def _sb_attn_call(q, k, v, u, tq):
    b, t, _ = q.shape
    w = SB_PAIRS * LANES
    return pl.pallas_call(
        functools.partial(_sb_attn_kernel, tq=tq),
        grid=(b, SB_WIDTH // w, t // tq),
        in_specs=[
            pl.BlockSpec((1, tq, w), lambda bi, p, i: (bi, i, p)),
            pl.BlockSpec((1, t, w), lambda bi, p, i: (bi, 0, p)),
            pl.BlockSpec((1, t, w), lambda bi, p, i: (bi, 0, p)),
            pl.BlockSpec((tq, tq), lambda bi, p, i: (0, 0)),
        ],
        out_specs=pl.BlockSpec((1, tq, w), lambda bi, p, i: (bi, i, p)),
        out_shape=jax.ShapeDtypeStruct((b, t, SB_WIDTH), BF16),
        compiler_params=pltpu.CompilerParams(
            dimension_semantics=("parallel", "parallel", "arbitrary"), vmem_limit_bytes=VMEM_LIMIT),
        name="sb_attention",
    )(q, k, v, u)


def _unit_lower_inverse(a, ri, ci):
    mm = lambda p_, q_: _bmm(p_, q_, NN, GDN_DOT_INVERSE)
    same = ri == ci
    x = None
    for level in range(1, 7):
        same_next = (ri >> level) == (ci >> level)
        off = jnp.where(jnp.logical_and(same_next, jnp.logical_not(same)), a, 0.0)
        if x is None:
            x = jnp.where(ri == ci, 1.0, 0.0) - off
        else:
            x = x - mm(x, mm(off, x))
        same = same_next
    return x


def _gdn_kernel(qkv_ref, z_ref, gbc_ref, gbr_ref, on_ref, o_ref, state, *, tt):
    c_sz = DN_CHUNK
    hd = DN_HEAD_DIM
    nc = tt // c_sz
    qkv_s = qkv_ref.at[0]

    @pl.when(pl.program_id(1) == 0)
    def _():
        state[...] = jnp.zeros_like(state)

    ri = lax.broadcasted_iota(jnp.int32, (c_sz, c_sz), 0)
    ci = lax.broadcasted_iota(jnp.int32, (c_sz, c_sz), 1)
    tril_incl = ci <= ri
    tril_strict = ci < ri

    gcol = gbc_ref[0].reshape(nc, c_sz, AB_PAD)
    heads = range(DN_HEADS)

    def cat(f):
        return jnp.concatenate([f(h) for h in heads], axis=0)

    q = cat(lambda h: qkv_s[:, h * hd:(h + 1) * hd].reshape(nc, c_sz, hd))
    k = cat(lambda h: qkv_s[:, DN_WIDTH + h * hd:DN_WIDTH + (h + 1) * hd].reshape(nc, c_sz, hd))
    v = cat(lambda h: qkv_s[:, 2 * DN_WIDTH + h * hd:2 * DN_WIDTH + (h + 1) * hd].reshape(nc, c_sz, hd))
    gc_col = cat(lambda h: gcol[:, :, h:h + 1])
    beta_col = cat(lambda h: gcol[:, :, DN_HEADS + h:DN_HEADS + h + 1])
    gc_row = gbr_ref[0, 0:DN_HEADS].reshape(DN_HEADS * nc, 1, c_sz)
    g_last = gc_row[:, :, c_sz - 1:c_sz]
    decay = jnp.where(tril_incl, jnp.exp(jnp.where(tril_incl, gc_col - gc_row, 0.0)), 0.0)
    k_beta = k * beta_col
    a = jnp.where(tril_strict, _bmm(k_beta, k, NT, GDN_DOT_SCORES) * decay, 0.0)
    t_mat = _unit_lower_inverse(a, ri, ci)
    rhs = jnp.concatenate([v * beta_col, k_beta * jnp.exp(gc_col)], axis=-1)
    uw = _bmm(t_mat, rhs, NN, GDN_DOT_SOLVE)
    intra = jnp.where(tril_incl, _bmm(q, k, NT, GDN_DOT_SCORES) * decay, 0.0)
    kd = k * jnp.exp(g_last - gc_col)
    iuw = _bmm(intra, uw, NN, GDN_DOT_CHUNK)
    kuw = _bmm(kd, uw, TN, GDN_DOT_CHUNK)
    q_eff = q * jnp.exp(gc_col) - iuw[:, :, hd:]
    o_base = iuw[:, :, :hd]
    e_last = jnp.exp(g_last)
    kw = kuw[:, :, hd:]
    n_mat = kuw[:, :, :hd]

    s_cur = [state[h] for h in heads]
    s_before = [[None] * nc for _ in heads]
    for c in range(nc):
        for h in heads:
            g = h * nc + c
            s_before[h][c] = s_cur[h]
            s_cur[h] = (e_last[g] * s_cur[h] - GDN_DOT_STATE(kw[g], s_cur[h])) + n_mat[g]
    for h in heads:
        state[h] = s_cur[h]
    s_all = jnp.stack([s_before[h][c] for h in heads for c in range(nc)])
    o_all = _bmm(q_eff, s_all, NN, GDN_DOT_OUT) + o_base
    outs = []
    for h in heads:
        o = o_all[h * nc:(h + 1) * nc].reshape(tt, hd)
        ms = jnp.mean(o * o, axis=-1, keepdims=True)
        zz = z_ref[0, :, h * hd:(h + 1) * hd]
        outs.append((o * lax.rsqrt(ms + EPS) * on_ref[...] * _silu(zz)).astype(BF16))
    o_ref[0] = jnp.concatenate(outs, axis=-1)


def _gdn_call(dn, z, gbc, gbr5, out_norm, tt):
    b, t, _ = dn.shape
    nc = tt // DN_CHUNK
    return pl.pallas_call(
        functools.partial(_gdn_kernel, tt=tt),
        grid=(b, t // tt),
        in_specs=[
            pl.BlockSpec((1, tt, 3 * DN_WIDTH), lambda bi, i: (bi, i, 0)),
            pl.BlockSpec((1, tt, DN_WIDTH), lambda bi, i: (bi, i, 0)),
            pl.BlockSpec((1, tt, AB_PAD), lambda bi, i: (bi, i, 0)),
            pl.BlockSpec((1, 2 * DN_HEADS, nc, 1, DN_CHUNK), lambda bi, i: (bi, 0, i, 0, 0)),
            pl.BlockSpec(out_norm.shape, lambda bi, i: (0, 0)),
        ],
        out_specs=pl.BlockSpec((1, tt, DN_WIDTH), lambda bi, i: (bi, i, 0)),
        out_shape=jax.ShapeDtypeStruct((b, t, DN_WIDTH), BF16),
        scratch_shapes=[
            pltpu.VMEM((DN_HEADS, DN_HEAD_DIM, DN_HEAD_DIM), F32),
        ],
        compiler_params=pltpu.CompilerParams(
            dimension_semantics=("parallel", "arbitrary"), vmem_limit_bytes=VMEM_LIMIT),
        name="gated_deltanet",
    )(dn, z, gbc, gbr5, out_norm)


def _post_kernel(x_ref, oa_ref, ob_ref, mod_ref, gain_ref, woa_ref, wob_ref, w1_ref, w2_ref, o_ref, *, ff_chunk):
    mix = _dot(oa_ref[0], woa_ref[...]) + _dot(ob_ref[0], wob_ref[...])
    x1 = x_ref[0] + mod_ref[2:3, :] * mix
    ms = jnp.mean(x1 * x1, axis=-1, keepdims=True)
    y = x1 * lax.rsqrt(ms + EPS) * gain_ref[...]
    hb = (y * (1.0 + mod_ref[4:5, :]) + mod_ref[3:4, :]).astype(BF16)
    d_ff = w1_ref.shape[1]
    acc = jnp.zeros(x1.shape, F32)
    for f in range(d_ff // ff_chunk):
        fs = slice(f * ff_chunk, (f + 1) * ff_chunk)
        hid = jnp.maximum(_dot(hb, w1_ref[:, fs]), 0.0)
        acc = acc + _dot((hid * hid).astype(BF16), w2_ref[fs, :])
    o_ref[0] = x1 + mod_ref[5:6, :] * acc


def _post_call(x, oa, ob, mod, l, gain, w_out, w1, w2, tm):
    b, t, d = x.shape
    tok = lambda w: pl.BlockSpec((1, tm, w), lambda bi, i: (bi, i, 0))
    lay = lambda a: _layer_spec(l, a.shape[1], a.shape[2])
    woa = pl.BlockSpec((None, SB_WIDTH, d), lambda bi, i: (l, 0, 0), pipeline_mode=pl.Buffered(1))
    wob = pl.BlockSpec((None, DN_WIDTH, d), lambda bi, i: (l, SB_WIDTH // DN_WIDTH, 0),
                       pipeline_mode=pl.Buffered(1))
    return pl.pallas_call(
        functools.partial(_post_kernel, ff_chunk=1024),
        grid=(b, t // tm),
        in_specs=[
            tok(d), tok(SB_WIDTH), tok(DN_WIDTH),
            pl.BlockSpec((None, None, N_MOD, d), lambda bi, i: (l, bi, 0, 0)),
            lay(gain), woa, wob, lay(w1), lay(w2),
        ],
        out_specs=tok(d),
        out_shape=jax.ShapeDtypeStruct((b, t, d), F32),
        compiler_params=pltpu.CompilerParams(
            dimension_semantics=("parallel", "parallel"), vmem_limit_bytes=VMEM_LIMIT),
        name="outproj_mlp",
    )(x, oa, ob, mod, gain, w_out, w_out, w1, w2)


def kernel(x, c, w_ada, b_ada, norm_mix, norm_mlp, w_in, sb_q_norm, sb_k_norm, conv_w, a_log, dt_bias,
           dn_out_norm, w_out, w_ff1, w_ff2):
    b, t, d = x.shape
    depth = w_ada.shape[0]
    tm = 512
    tq = 256
    tt = 512
    cum_rows = 256

    mod = _ada_call(c, w_ada, b_ada).reshape(depth, b, N_MOD, d)

    hi = jnp.arange(SB_WIDTH) // SB_HEAD_DIM
    blk = jnp.where(hi[:, None] == hi[None, :], 1.0 / SB_HEAD_DIM, 0.0).astype(BF16)
    ti = jnp.arange(cum_rows)
    cum = ((ti[:, None] // DN_CHUNK == ti[None, :] // DN_CHUNK) & (ti[None, :] <= ti[:, None])).astype(BF16)
    qi = jnp.arange(tq)
    u = (qi[:, None] > qi[None, :]).astype(BF16)

    ab0 = 3 * SB_WIDTH + 4 * DN_WIDTH
    n_ab = w_in.shape[2] - ab0
    w_in_b = w_in.astype(BF16)
    wab = jnp.pad(w_in[:, :, ab0:], ((0, 0), (0, 0), (0, AB_PAD - n_ab))).astype(BF16)
    w_out_b, w1_b, w2_b = w_out.astype(BF16), w_ff1.astype(BF16), w_ff2.astype(BF16)
    qg = jnp.tile(sb_q_norm, (1, SB_HEADS))[:, None, :]
    kg = jnp.tile(sb_k_norm, (1, SB_HEADS))[:, None, :]
    alog = jnp.pad(a_log, ((0, 0), (0, AB_PAD - DN_HEADS)))[:, None, :]
    dtb = jnp.pad(dt_bias, ((0, 0), (0, AB_PAD - DN_HEADS)))[:, None, :]
    gain_mix, gain_mlp = norm_mix[:, None, :], norm_mlp[:, None, :]

    for l in range(depth):
        q, k, v, dn, z, gbc, gbr = _premix_call(x, mod, l, gain_mix, w_in_b, wab, qg, kg, blk, alog, dtb, cum,
                                                conv_w, tm)
        oa = _sb_attn_call(q, k, v, u, tq)
        gbr5 = gbr.reshape(b, 2 * DN_HEADS, t // DN_CHUNK, 1, DN_CHUNK)
        ob = _gdn_call(dn, z, gbc, gbr5, dn_out_norm[l][None, :], tt)
        x = _post_call(x, oa, ob, mod, l, gain_mlp, w_out_b, w1_b, w2_b, tm)
    return x
```

```python
import functools

import jax
import jax.numpy as jnp
from jax import lax
from jax.experimental import pallas as pl
from jax.experimental.pallas import tpu as pltpu

F32 = jnp.float32
BF16 = jnp.bfloat16
EPS = 1e-6

SB_HEADS = 8
SB_HEAD_DIM = 64
SB_WIDTH = SB_HEADS * SB_HEAD_DIM
DN_HEADS = 4
DN_HEAD_DIM = 128
DN_WIDTH = DN_HEADS * DN_HEAD_DIM
DN_CHUNK = 64
CONV_WIDTH = 4
N_MOD = 6
LANES = 128
AB_PAD = LANES
VMEM_LIMIT = 56 * 1024 * 1024
LOG2E = 1.4426950408889634
SB_SKIP_LOG2 = 127.0

NN = (((1,), (0,)), ((), ()))
NT = (((1,), (1,)), ((), ()))
TN = (((0,), (0,)), ((), ()))


def _dot(a, b, dims=NN):
    return lax.dot_general(a, b, dims, preferred_element_type=F32)


def _split2(a):
    hi = a.astype(BF16)
    lo = (a - hi.astype(F32)).astype(BF16)
    return hi, lo


def _split3(a):
    hi = a.astype(BF16)
    r = a - hi.astype(F32)
    mid = r.astype(BF16)
    lo = (r - mid.astype(F32)).astype(BF16)
    return hi, mid, lo


def _dot_x3(a, b, dims=NN):
    ah, al = _split2(a)
    bh, bl = _split2(b)
    return _dot(ah, bh, dims) + (_dot(ah, bl, dims) + _dot(al, bh, dims))


def _dot_x1(a, b, dims=NN):
    return _dot(a.astype(BF16), b.astype(BF16), dims)


def _bmm(a, b, dims, dot):
    return jnp.stack([dot(a[g], b[g], dims) for g in range(a.shape[0])])


GDN_DOT_SCORES = _dot_x1
GDN_DOT_INVERSE = _dot_x1
GDN_DOT_SOLVE = _dot_x1
GDN_DOT_CHUNK = _dot_x1
GDN_DOT_STATE = _dot_x1
GDN_DOT_OUT = _dot_x1


def _sigmoid(x):
    return 1.0 / (1.0 + jnp.exp(-x))


def _silu(x):
    h = 0.5 * x
    return h + h * jnp.tanh(h)


def _softplus(x):
    return jnp.maximum(x, 0.0) + jnp.log1p(jnp.exp(-jnp.abs(x)))


def _ada_kernel(c_ref, w_ref, b_ref, o_ref):
    c = c_ref[...]
    cond = c * _sigmoid(c)
    o_ref[0] = _dot_x3(cond, w_ref[0]) + b_ref[0]


def _ada_call(c, w_ada, b_ada):
    depth, d, n = w_ada.shape
    b = c.shape[0]
    tn = 1536
    return pl.pallas_call(
        _ada_kernel,
        grid=(depth, n // tn),
        in_specs=[
            pl.BlockSpec((b, d), lambda l, j: (0, 0)),
            pl.BlockSpec((1, d, tn), lambda l, j: (l, 0, j)),
            pl.BlockSpec((1, 1, tn), lambda l, j: (l, 0, j)),
        ],
        out_specs=pl.BlockSpec((1, b, tn), lambda l, j: (l, 0, j)),
        out_shape=jax.ShapeDtypeStruct((depth, b, n), F32),
        compiler_params=pltpu.CompilerParams(
            dimension_semantics=("arbitrary", "arbitrary"), vmem_limit_bytes=VMEM_LIMIT),
        name="adaln_mod",
    )(c, w_ada, b_ada.reshape(depth, 1, n))


def _premix_kernel(x_ref, mod_ref, gain_ref, wsb_ref, wdn_ref, wz_ref, wab_ref, qg_ref, kg_ref,
                   blk_ref, alog_ref, dtb_ref, cum_ref, cw_ref,
                   q_ref, k_ref, v_ref, dn_ref, z_ref, gbc_ref, gbr_ref, xbuf):
    tm = x_ref.shape[1]
    x = x_ref[0]
    ms = jnp.mean(x * x, axis=-1, keepdims=True)
    y = x * lax.rsqrt(ms + EPS) * gain_ref[...]
    h = y * (1.0 + mod_ref[1:2, :]) + mod_ref[0:1, :]
    hb = h.astype(BF16)

    @pl.when(pl.program_id(1) == 0)
    def _():
        xbuf[0:8, :] = jnp.zeros((8, 3 * DN_WIDTH), F32)

    hd = DN_HEAD_DIM
    gw = 2 * hd
    n_groups = 3 * DN_WIDTH // gw

    def dn_group(j):
        xbuf[8:8 + tm, j * gw:(j + 1) * gw] = _dot(hb, wdn_ref[:, j * gw:(j + 1) * gw])

    dn_group(0)
    sb_groups = []
    ab = None
    for j in range(n_groups):
        if j + 1 < n_groups:
            dn_group(j + 1)
        sb_groups.append(_dot(hb, wsb_ref[:, j * gw:(j + 1) * gw]))
        if j < DN_WIDTH // gw:
            z_ref[0, :, j * gw:(j + 1) * gw] = _dot(hb, wz_ref[:, j * gw:(j + 1) * gw])
        elif ab is None:
            ab = _dot(hb, wab_ref[...])
        for cb in (2 * j, 2 * j + 1):
            ls = slice(cb * hd, (cb + 1) * hd)
            acc = cw_ref[CONV_WIDTH - 1:CONV_WIDTH, ls] * xbuf[8:8 + tm, ls]
            for kk in range(CONV_WIDTH - 1):
                off = 8 - (CONV_WIDTH - 1) + kk
                acc = acc + cw_ref[kk:kk + 1, ls] * xbuf[off:off + tm, ls]
            act = _silu(acc)
            if cb < 2 * DN_HEADS:
                act = act * lax.rsqrt(jnp.sum(act * act, axis=-1, keepdims=True) + EPS)
            if cb < DN_HEADS:
                act = act * (hd ** -0.5)
            dn_ref[0, :, ls] = act
    xbuf[0:8, :] = xbuf[tm:tm + 8, :]
    sb = jnp.concatenate(sb_groups, axis=-1)

    blk = blk_ref[...]

    def head_rmsnorm(a, g):
        m = _dot((a * a).astype(BF16), blk)
        return a * lax.rsqrt(m + EPS) * g

    q = head_rmsnorm(sb[:, :SB_WIDTH], qg_ref[...]) * (SB_HEAD_DIM ** -0.5 * LOG2E)
    k = head_rmsnorm(sb[:, SB_WIDTH:2 * SB_WIDTH], kg_ref[...])
    q_ref[0] = q.astype(BF16)
    k_ref[0] = k.astype(BF16)
    v_ref[0] = sb[:, 2 * SB_WIDTH:].astype(BF16)

    lane =lax.broadcasted_iota(jnp.int32, ab.shape, 1)
    g = -jnp.exp(alog_ref[...]) * _softplus(ab + dtb_ref[...])
    cum = cum_ref[...]
    rows = cum.shape[0]
    gcs = []
    for r0 in range(0, g.shape[0], rows):
        g1, g2, g3 = _split3(g[r0:r0 + rows])
        gcs.append(_dot(cum, g1) + (_dot(cum, g2) + _dot(cum, g3)))
    gc = jnp.concatenate(gcs, axis=0)
    gb = jnp.where(lane < DN_HEADS, gc, _sigmoid(ab))
    gbc_ref[0] = gb
    gbr_ref[0] = gb.T[0:2 * DN_HEADS, :]


def _layer_spec(l, rows, cols, col_block=0):
    return pl.BlockSpec((None, rows, cols), lambda bi, i: (l, 0, col_block), pipeline_mode=pl.Buffered(1))


def _premix_call(x, mod, l, gain, w_in, wab, qg, kg, blk, alog, dtb, cum, conv_w, tm):
    b, t, d = x.shape
    full = lambda a: pl.BlockSpec(a.shape, lambda bi, i: (0,) * a.ndim, pipeline_mode=pl.Buffered(1))
    tok = lambda w: pl.BlockSpec((1, tm, w), lambda bi, i: (bi, i, 0))
    lay = lambda a: _layer_spec(l, a.shape[1], a.shape[2])
    sbw, dnw = 3 * SB_WIDTH, 3 * DN_WIDTH
    assert sbw == dnw and (sbw + dnw) % DN_WIDTH == 0
    return pl.pallas_call(
        _premix_kernel,
        grid=(b, t // tm),
        in_specs=[
            tok(d),
            pl.BlockSpec((None, None, N_MOD, d), lambda bi, i: (l, bi, 0, 0)),
            lay(gain), _layer_spec(l, d, sbw, 0), _layer_spec(l, d, dnw, 1),
            _layer_spec(l, d, DN_WIDTH, (sbw + dnw) // DN_WIDTH), lay(wab), lay(qg), lay(kg), full(blk),
            lay(alog), lay(dtb), full(cum), lay(conv_w),
        ],
        out_specs=[tok(SB_WIDTH), tok(SB_WIDTH), tok(SB_WIDTH), tok(3 * DN_WIDTH), tok(DN_WIDTH),
                   tok(AB_PAD),
                   pl.BlockSpec((1, 2 * DN_HEADS, tm), lambda bi, i: (bi, 0, i))],
        out_shape=[
            jax.ShapeDtypeStruct((b, t, SB_WIDTH), BF16),
            jax.ShapeDtypeStruct((b, t, SB_WIDTH), BF16),
            jax.ShapeDtypeStruct((b, t, SB_WIDTH), BF16),
            jax.ShapeDtypeStruct((b, t, 3 * DN_WIDTH), F32),
            jax.ShapeDtypeStruct((b, t, DN_WIDTH), F32),
            jax.ShapeDtypeStruct((b, t, AB_PAD), F32),
            jax.ShapeDtypeStruct((b, 2 * DN_HEADS, t), F32),
        ],
        scratch_shapes=[pltpu.VMEM((tm + 8, 3 * DN_WIDTH), F32)],
        compiler_params=pltpu.CompilerParams(
            dimension_semantics=("parallel", "arbitrary"), vmem_limit_bytes=VMEM_LIMIT),
        name="premix_proj",
    )(x, mod, gain, w_in, w_in, w_in, wab, qg, kg, blk, alog, dtb, cum, conv_w)


SB_PAIRS = 4


def _sb_softplus2(z):
    return jnp.maximum(z, 0.0) + jnp.log(1.0 + jnp.exp2(-jnp.abs(z))) * LOG2E


def _sb_attn_kernel(q_ref, k_ref, v_ref, u_ref, o_ref, *, tq):
    i = pl.program_id(2)
    lane = lax.broadcasted_iota(jnp.int32, (1, LANES), 1)
    first = lane < SB_HEAD_DIM
    q2s, lanes = [], []
    for p in range(SB_PAIRS):
        q = q_ref[0, :, p * LANES:(p + 1) * LANES]
        zero = jnp.zeros_like(q)
        q2s.append(jnp.concatenate([jnp.where(first, q, zero), jnp.where(first, zero, q)], axis=0))
        lanes.append(slice(p * LANES, (p + 1) * LANES))
    n_heads = 2 * SB_PAIRS
    u = u_ref[...]
    hq = tq // 2
    tri = (lax.broadcasted_iota(jnp.int32, (hq, hq), 1)
           < lax.broadcasted_iota(jnp.int32, (hq, hq), 0))
    zeros_q = jnp.zeros((hq, hq), BF16)

    def quadrants(a):
        return a[:hq, :hq], a[hq:, :hq], a[hq:, hq:]

    def diag_tile(tl, bl, br):
        tl = jnp.where(tri, tl, 0.0).astype(BF16)
        br = jnp.where(tri, br, 0.0).astype(BF16)
        return jnp.concatenate([jnp.concatenate([tl, zeros_q], axis=1),
                                jnp.concatenate([bl.astype(BF16), br], axis=1)], axis=0)

    def kv_block(j, ls):
        start = pl.multiple_of(j * tq, tq)
        return k_ref[0, pl.ds(start, tq), ls], v_ref[0, pl.ds(start, tq), ls]

    def scores(j):
        zs, vbs = [], []
        for p in range(SB_PAIRS):
            kb, vb = kv_block(j, lanes[p])
            z2 = _dot(q2s[p], kb, NT)
            zs += [z2[:tq], z2[tq:]]
            vbs.append(vb)
        return zs, vbs

    def suffix_sums(spbs):
        suf = _dot(jnp.concatenate(spbs, axis=0), u)
        sufs, r0 = [], 0
        for spb in spbs:
            sufs.append(suf[r0:r0 + spb.shape[0]])
            r0 += spb.shape[0]
        tots = [sf[:, 0:1] + spb[:, 0:1].astype(F32) for sf, spb in zip(sufs, spbs)]
        return sufs, tots

    def weighted_values(atts, vbs):
        pvs = []
        for p in range(SB_PAIRS):
            rows = atts[2 * p].shape[0]
            pv2 = _dot(jnp.concatenate(atts[2 * p:2 * p + 2], axis=0), vbs[p])
            pvs += [pv2[:rows], pv2[rows:]]
        return pvs

    def prev_rows(zs, carries, vbs):
        sps = [_sb_softplus2(z) for z in zs]
        sufs, tots = suffix_sums([sp.astype(BF16) for sp in sps])
        atts = [jnp.exp2((z - sp) - suf - cc).astype(BF16) for z, sp, suf, cc in zip(zs, sps, sufs, carries)]
        return weighted_values(atts, vbs), tots

    has_prev = jnp.where(i > 0, 1.0, 0.0)
    z_d, v_d = scores(i)
    z_p, v_p = scores(jnp.maximum(i - 1, 0))
    zq_d = [quadrants(z) for z in z_d]
    spq_d = [[_sb_softplus2(zz) for zz in zq] for zq in zq_d]
    lbq_d = [[zz - sp for zz, sp in zip(zq, spq)] for zq, spq in zip(zq_d, spq_d)]
    sp_p = [_sb_softplus2(z) for z in z_p]
    lb_p = [z - sp for z, sp in zip(z_p, sp_p)]
    sufs, tots = suffix_sums([diag_tile(*spq) for spq in spq_d] + [sp.astype(BF16) for sp in sp_p])
    suf_d, suf_p = sufs[:n_heads], sufs[n_heads:]
    tot_d, tot_p = tots[:n_heads], tots[n_heads:]
    att_d = [diag_tile(*[jnp.exp2(lb - sf) for lb, sf in zip(lbq, quadrants(suf))])
             for lbq, suf in zip(lbq_d, suf_d)]
    att_p = [jnp.exp2(lb - suf - tot).astype(BF16) for lb, suf, tot in zip(lb_p, suf_p, tot_d)]
    pv_d = weighted_values(att_d, v_d)
    pv_p = weighted_values(att_p, v_p)
    accs = tuple(d + has_prev * p for d, p in zip(pv_d, pv_p))
    cs = tuple(d + has_prev * p for d, p in zip(tot_d, tot_p))

    def cond(st):
        j, _, c = st
        cmin = c[0]
        for cc in c[1:]:
            cmin = jnp.minimum(cmin, cc)
        return jnp.logical_and(j >= 0, jnp.min(cmin) < SB_SKIP_LOG2)

    def body(st):
        j, a, c = st
        zs, vbs = scores(j)
        pvs, tots = prev_rows(zs, c, vbs)
        return (j - 1, tuple(aa + pv for aa, pv in zip(a, pvs)),
                tuple(cc + tot for cc, tot in zip(c, tots)))

    _, accs, _ = lax.while_loop(cond, body, (i - 2, accs, cs))
    for p in range(SB_PAIRS):
        o_ref[0, :, p * LANES:(p + 1) * LANES] = jnp.where(first, accs[2 * p], accs[2 * p + 1]).astype(BF16)


def _sb_attn_call(q, k, v, u, tq):
    b, t, _ = q.shape
    w = SB_PAIRS * LANES
    return pl.pallas_call(
        functools.partial(_sb_attn_kernel, tq=tq),
        grid=(b, SB_WIDTH // w, t // tq),
        in_specs=[
            pl.BlockSpec((1, tq, w), lambda bi, p, i: (bi, i, p)),
            pl.BlockSpec((1, t, w), lambda bi, p, i: (bi, 0, p)),
            pl.BlockSpec((1, t, w), lambda bi, p, i: (bi, 0, p)),
            pl.BlockSpec((tq, tq), lambda bi, p, i: (0, 0)),
        ],
        out_specs=pl.BlockSpec((1, tq, w), lambda bi, p, i: (bi, i, p)),
        out_shape=jax.ShapeDtypeStruct((b, t, SB_WIDTH), BF16),
        compiler_params=pltpu.CompilerParams(
            dimension_semantics=("parallel", "parallel", "arbitrary"), vmem_limit_bytes=VMEM_LIMIT),
        name="sb_attention",
    )(q, k, v, u)


def _unit_lower_inverse(a, ri, ci):
    mm = lambda p_, q_: _bmm(p_, q_, NN, GDN_DOT_INVERSE)
    same = ri == ci
    x = None
    for level in range(1, 7):
        same_next = (ri >> level) == (ci >> level)
        off = jnp.where(jnp.logical_and(same_next, jnp.logical_not(same)), a, 0.0)
        if x is None:
            x = jnp.where(ri == ci, 1.0, 0.0) - off
        else:
            x = x - mm(x, mm(off, x))
        same = same_next
    return x


def _gdn_kernel(qkv_ref, z_ref, gbc_ref, gbr_ref, on_ref, o_ref, state, *, tt):
    c_sz = DN_CHUNK
    hd = DN_HEAD_DIM
    nc = tt // c_sz
    qkv_s = qkv_ref.at[0]

    @pl.when(pl.program_id(1) == 0)
    def _():
        state[...] = jnp.zeros_like(state)

    ri = lax.broadcasted_iota(jnp.int32, (c_sz, c_sz), 0)
    ci = lax.broadcasted_iota(jnp.int32, (c_sz, c_sz), 1)
    tril_incl = ci <= ri
    tril_strict = ci < ri

    gcol = gbc_ref[0].reshape(nc, c_sz, AB_PAD)
    heads = range(DN_HEADS)

    def cat(f):
        return jnp.concatenate([f(h) for h in heads], axis=0)

    q = cat(lambda h: qkv_s[:, h * hd:(h + 1) * hd].reshape(nc, c_sz, hd))
    k = cat(lambda h: qkv_s[:, DN_WIDTH + h * hd:DN_WIDTH + (h + 1) * hd].reshape(nc, c_sz, hd))
    v = cat(lambda h: qkv_s[:, 2 * DN_WIDTH + h * hd:2 * DN_WIDTH + (h + 1) * hd].reshape(nc, c_sz, hd))
    gc_col = cat(lambda h: gcol[:, :, h:h + 1])
    beta_col = cat(lambda h: gcol[:, :, DN_HEADS + h:DN_HEADS + h + 1])
    gc_row = gbr_ref[0, 0:DN_HEADS].reshape(DN_HEADS * nc, 1, c_sz)
    g_last = gc_row[:, :, c_sz - 1:c_sz]
    decay = jnp.where(tril_incl, jnp.exp(jnp.where(tril_incl, gc_col - gc_row, 0.0)), 0.0)
    k_beta = k * beta_col
    a = jnp.where(tril_strict, _bmm(k_beta, k, NT, GDN_DOT_SCORES) * decay, 0.0)
    t_mat = _unit_lower_inverse(a, ri, ci)
    rhs = jnp.concatenate([v * beta_col, k_beta * jnp.exp(gc_col)], axis=-1)
    uw = _bmm(t_mat, rhs, NN, GDN_DOT_SOLVE)
    intra = jnp.where(tril_incl, _bmm(q, k, NT, GDN_DOT_SCORES) * decay, 0.0)
    kd = k * jnp.exp(g_last - gc_col)
    iuw = _bmm(intra, uw, NN, GDN_DOT_CHUNK)
    kuw = _bmm(kd, uw, TN, GDN_DOT_CHUNK)
    q_eff = q * jnp.exp(gc_col) - iuw[:, :, hd:]
    o_base = iuw[:, :, :hd]
    e_last = jnp.exp(g_last)
    kw = kuw[:, :, hd:]
    n_mat = kuw[:, :, :hd]

    s_cur = [state[h] for h in heads]
    s_before = [[None] * nc for _ in heads]
    for c in range(nc):
        for h in heads:
            g = h * nc + c
            s_before[h][c] = s_cur[h]
            s_cur[h] = (e_last[g] * s_cur[h] - GDN_DOT_STATE(kw[g], s_cur[h])) + n_mat[g]
    for h in heads:
        state[h] = s_cur[h]
    s_all = jnp.stack([s_before[h][c] for h in heads for c in range(nc)])
    o_all = _bmm(q_eff, s_all, NN, GDN_DOT_OUT) + o_base
    outs = []
    for h in heads:
        o = o_all[h * nc:(h + 1) * nc].reshape(tt, hd)
        ms = jnp.mean(o * o, axis=-1, keepdims=True)
        zz = z_ref[0, :, h * hd:(h + 1) * hd]
        outs.append((o * lax.rsqrt(ms + EPS) * on_ref[...] * _silu(zz)).astype(BF16))
    o_ref[0] = jnp.concatenate(outs, axis=-1)


def _gdn_call(dn, z, gbc, gbr5, out_norm, tt):
    b, t, _ = dn.shape
    nc = tt // DN_CHUNK
    return pl.pallas_call(
        functools.partial(_gdn_kernel, tt=tt),
        grid=(b, t // tt),
        in_specs=[
            pl.BlockSpec((1, tt, 3 * DN_WIDTH), lambda bi, i: (bi, i, 0)),
            pl.BlockSpec((1, tt, DN_WIDTH), lambda bi, i: (bi, i, 0)),
            pl.BlockSpec((1, tt, AB_PAD), lambda bi, i: (bi, i, 0)),
            pl.BlockSpec((1, 2 * DN_HEADS, nc, 1, DN_CHUNK), lambda bi, i: (bi, 0, i, 0, 0)),
            pl.BlockSpec(out_norm.shape, lambda bi, i: (0, 0)),
        ],
        out_specs=pl.BlockSpec((1, tt, DN_WIDTH), lambda bi, i: (bi, i, 0)),
        out_shape=jax.ShapeDtypeStruct((b, t, DN_WIDTH), BF16),
        scratch_shapes=[
            pltpu.VMEM((DN_HEADS, DN_HEAD_DIM, DN_HEAD_DIM), F32),
        ],
        compiler_params=pltpu.CompilerParams(
            dimension_semantics=("parallel", "arbitrary"), vmem_limit_bytes=VMEM_LIMIT),
        name="gated_deltanet",
    )(dn, z, gbc, gbr5, out_norm)


def _post_kernel(x_ref, oa_ref, ob_ref, mod_ref, gain_ref, woa_ref, wob_ref, w1_ref, w2_ref, o_ref, *, ff_chunk):
    mix = _dot(oa_ref[0], woa_ref[...]) + _dot(ob_ref[0], wob_ref[...])
    x1 = x_ref[0] + mod_ref[2:3, :] * mix
    ms = jnp.mean(x1 * x1, axis=-1, keepdims=True)
    y = x1 * lax.rsqrt(ms + EPS) * gain_ref[...]
    hb = (y * (1.0 + mod_ref[4:5, :]) + mod_ref[3:4, :]).astype(BF16)
    d_ff = w1_ref.shape[1]
    acc = jnp.zeros(x1.shape, F32)
    for f in range(d_ff // ff_chunk):
        fs = slice(f * ff_chunk, (f + 1) * ff_chunk)
        hid = jnp.maximum(_dot(hb, w1_ref[:, fs]), 0.0)
        acc = acc + _dot((hid * hid).astype(BF16), w2_ref[fs, :])
    o_ref[0] = x1 + mod_ref[5:6, :] * acc


def _post_call(x, oa, ob, mod, l, gain, w_out, w1, w2, tm):
    b, t, d = x.shape
    tok = lambda w: pl.BlockSpec((1, tm, w), lambda bi, i: (bi, i, 0))
    lay = lambda a: _layer_spec(l, a.shape[1], a.shape[2])
    woa = pl.BlockSpec((None, SB_WIDTH, d), lambda bi, i: (l, 0, 0), pipeline_mode=pl.Buffered(1))
    wob = pl.BlockSpec((None, DN_WIDTH, d), lambda bi, i: (l, SB_WIDTH // DN_WIDTH, 0),
                       pipeline_mode=pl.Buffered(1))
    return pl.pallas_call(
        functools.partial(_post_kernel, ff_chunk=1024),
        grid=(b, t // tm),
        in_specs=[
            tok(d), tok(SB_WIDTH), tok(DN_WIDTH),
            pl.BlockSpec((None, None, N_MOD, d), lambda bi, i: (l, bi, 0, 0)),
            lay(gain), woa, wob, lay(w1), lay(w2),
        ],
        out_specs=tok(d),
        out_shape=jax.ShapeDtypeStruct((b, t, d), F32),
        compiler_params=pltpu.CompilerParams(
            dimension_semantics=("parallel", "parallel"), vmem_limit_bytes=VMEM_LIMIT),
        name="outproj_mlp",
    )(x, oa, ob, mod, gain, w_out, w_out, w1, w2)


def kernel(x, c, w_ada, b_ada, norm_mix, norm_mlp, w_in, sb_q_norm, sb_k_norm, conv_w, a_log, dt_bias,
           dn_out_norm, w_out, w_ff1, w_ff2):
    b, t, d = x.shape
    depth = w_ada.shape[0]
    tm = 512
    tq = 256
    tt = 512
    cum_rows = 256

    mod = _ada_call(c, w_ada, b_ada).reshape(depth, b, N_MOD, d)

    hi = jnp.arange(SB_WIDTH) // SB_HEAD_DIM
    blk = jnp.where(hi[:, None] == hi[None, :], 1.0 / SB_HEAD_DIM, 0.0).astype(BF16)
    ti = jnp.arange(cum_rows)
    cum = ((ti[:, None] // DN_CHUNK == ti[None, :] // DN_CHUNK) & (ti[None, :] <= ti[:, None])).astype(BF16)
    qi = jnp.arange(tq)
    u = (qi[:, None] > qi[None, :]).astype(BF16)

    ab0 = 3 * SB_WIDTH + 4 * DN_WIDTH
    n_ab = w_in.shape[2] - ab0
    w_in_b = w_in.astype(BF16)
    wab = jnp.pad(w_in[:, :, ab0:], ((0, 0), (0, 0), (0, AB_PAD - n_ab))).astype(BF16)
    w_out_b, w1_b, w2_b = w_out.astype(BF16), w_ff1.astype(BF16), w_ff2.astype(BF16)
    qg = jnp.tile(sb_q_norm, (1, SB_HEADS))[:, None, :]
    kg = jnp.tile(sb_k_norm, (1, SB_HEADS))[:, None, :]
    alog = jnp.pad(a_log, ((0, 0), (0, AB_PAD - DN_HEADS)))[:, None, :]
    dtb = jnp.pad(dt_bias, ((0, 0), (0, AB_PAD - DN_HEADS)))[:, None, :]
    gain_mix, gain_mlp = norm_mix[:, None, :], norm_mlp[:, None, :]

    for l in range(depth):
        q, k, v, dn, z, gbc, gbr = _premix_call(x, mod, l, gain_mix, w_in_b, wab, qg, kg, blk, alog, dtb, cum,
                                                conv_w, 1024)
        oa = _sb_attn_call(q, k, v, u, tq)
        gbr5 = gbr.reshape(b, 2 * DN_HEADS, t // DN_CHUNK, 1, DN_CHUNK)
        ob = _gdn_call(dn, z, gbc, gbr5, dn_out_norm[l][None, :], tt)
        x = _post_call(x, oa, ob, mod, l, gain_mlp, w_out_b, w1_b, w2_b, 1024)
    return x
```
